```python
import jax, jax.numpy as jnp
from jax import lax
import numpy as np

D_MODEL = 1024
BATCH = 4
SEQ = 4096
DEPTH = 4

D_MIX = D_MODEL
HG_HEADS = 4
HG_DK = 128
HG_DV = 128
HG_W = HG_HEADS * HG_DV
HG_CHUNK = 64
NSA_HEADS = 8
NSA_KV_HEADS = 2
NSA_DH = 64
NSA_G = NSA_HEADS // NSA_KV_HEADS
NSA_W = NSA_HEADS * NSA_DH
KV_W = NSA_KV_HEADS * NSA_DH
CMP_BLOCK = 32
CMP_STRIDE = 16
CMP_HIDDEN = 128
SLC_BLOCK = 64
SLC_TOPK = 16
SLC_Q_BLOCK = 64
WINDOW = 512
Q_BLOCK = 128
FORCE_SCORE = 1e4
ROPE_THETA = 500000.0
ROPE_DIM = NSA_DH // 4
PLE_DIM = 256
RMS_EPS = 1e-6
IN_SPLITS = (HG_W,) * 5 + (NSA_W,) + (KV_W,) * 6 + (3 * NSA_HEADS, NSA_W)
IN_COLS = 5 * HG_W + NSA_W + 6 * KV_W + 3 * NSA_HEADS + NSA_W

kernel_name = "hymba_hgrn2_nsa_trunk"


def rmsnorm(x, g):
    xf = x.astype(jnp.float32)
    var = jnp.mean(xf * xf, axis=-1, keepdims=True)
    return (xf * lax.rsqrt(var + RMS_EPS)).astype(x.dtype) * g


def masked_softmax(s, mask):
    s = jnp.where(mask, s.astype(jnp.float32), -jnp.inf)
    m = jnp.max(s, axis=-1, keepdims=True)
    m = jnp.where(jnp.isfinite(m), m, 0.0)
    e = jnp.where(mask, jnp.exp(s - m), 0.0)
    return e / jnp.maximum(jnp.sum(e, axis=-1, keepdims=True), 1e-30)


def rope_tables(seq):
    pos = jnp.arange(seq, dtype=jnp.float32)
    inv = ROPE_THETA ** (-jnp.arange(0, ROPE_DIM, 2, dtype=jnp.float32) / ROPE_DIM)
    ang = pos[:, None] * inv[None, :]
    return jnp.cos(ang), jnp.sin(ang)


def partial_rope(x, cos, sin):
    half = ROPE_DIM // 2
    x1, x2, xp = x[..., :half], x[..., half:ROPE_DIM], x[..., ROPE_DIM:]
    c = cos.astype(x.dtype)
    s = sin.astype(x.dtype)
    return jnp.concatenate([x1 * c - x2 * s, x1 * s + x2 * c, xp], axis=-1)


def heads(a, n):
    B, T, W = a.shape
    return a.reshape(B, T, n, W // n).transpose(0, 2, 1, 3)


def hgrn2_chunkwise(q, log_f, k, v):
    B, H, T, DK = q.shape
    DV = v.shape[-1]
    C = HG_CHUNK
    N = T // C

    def to_chunks(a):
        return jnp.moveaxis(a.reshape(B, H, N, C, a.shape[-1]), 2, 0)

    causal = jnp.tril(jnp.ones((C, C), dtype=bool))[:, :, None]

    def step(S, inp):
        qi, lfi, ki, vi = inp
        b = jnp.cumsum(lfi.astype(jnp.float32), axis=-2)
        diff = b[..., :, None, :] - b[..., None, :, :]
        decay = jnp.exp(jnp.where(causal, diff, -jnp.inf))
        A = jnp.einsum('bhtd,bhsd,bhtsd->bhts', qi, ki, decay)
        o = jnp.einsum('bhts,bhsv->bhtv', A, vi) + jnp.einsum('bhtd,bhdv->bhtv', qi * jnp.exp(b), S)
        b_last = b[..., -1:, :]
        S_new = jnp.exp(b_last[..., 0, :])[..., None] * S + jnp.einsum(
            'bhsd,bhsv->bhdv', ki * jnp.exp(b_last - b), vi)
        return S_new, o

    S0 = jnp.zeros((B, H, DK, DV), jnp.float32)
    _, o = lax.scan(step, S0, (to_chunks(q), to_chunks(log_f), to_chunks(k), to_chunks(v)))
    return jnp.moveaxis(o, 0, 2).reshape(B, H, T, DV).astype(v.dtype)


def compress(kv, pe, w1, w2):
    B, G, T, dh = kv.shape
    NC = (T - CMP_BLOCK) // CMP_STRIDE + 1
    idx = jnp.arange(NC)[:, None] * CMP_STRIDE + jnp.arange(CMP_BLOCK)[None, :]
    blocks = kv[:, :, idx, :] + pe
    flat = blocks.reshape(B, G, NC, CMP_BLOCK * dh)
    return jax.nn.silu(flat @ w1) @ w2


def window_attn(q, k, v, scale):
    B, KVH, G, T, dh = q.shape
    NB = T // Q_BLOCK
    NW = WINDOW // Q_BLOCK
    pad = ((0, 0), (0, 0), (WINDOW, 0), (0, 0))
    kb = jnp.pad(k, pad).reshape(B, KVH, NB + NW, Q_BLOCK, dh)
    vb = jnp.pad(v, pad).reshape(B, KVH, NB + NW, Q_BLOCK, dh)
    band_k = jnp.concatenate([kb[:, :, i:i + NB] for i in range(NW + 1)], axis=3)
    band_v = jnp.concatenate([vb[:, :, i:i + NB] for i in range(NW + 1)], axis=3)
    qb = q.reshape(B, KVH, G, NB, Q_BLOCK, dh)
    s = jnp.einsum('bgnxqd,bgxkd->bgnxqk', qb, band_k) * scale
    qpos = jnp.arange(NB)[:, None] * Q_BLOCK + jnp.arange(Q_BLOCK)[None, :]
    kpos = jnp.arange(NB)[:, None] * Q_BLOCK - WINDOW + jnp.arange((NW + 1) * Q_BLOCK)[None, :]
    d = qpos[:, :, None] - kpos[:, None, :]
    mask = (d >= 0) & (d < WINDOW) & (kpos[:, None, :] >= 0)
    p = masked_softmax(s, mask)
    o = jnp.einsum('bgnxqk,bgxkd->bgnxqd', p.astype(v.dtype), band_v)
    return o.reshape(B, KVH, G, T, dh)


def selected_attn(q, k, v, sel, scale):
    B, KVH, G, T, dh = q.shape
    K = sel.shape[-1]
    NB = T // SLC_Q_BLOCK
    qb = jnp.moveaxis(q.reshape(B, KVH, G, NB, SLC_Q_BLOCK, dh), 3, 0)
    sb = jnp.moveaxis(sel.reshape(B, KVH, NB, SLC_Q_BLOCK, K), 2, 0)
    t0 = jnp.arange(NB) * SLC_Q_BLOCK
    bi = jnp.arange(B)[:, None, None, None]
    gi = jnp.arange(KVH)[None, :, None, None]

    def one(args):
        qx, sx, s0 = args
        tok = (sx[..., None] * SLC_BLOCK + jnp.arange(SLC_BLOCK)).reshape(B, KVH, SLC_Q_BLOCK, K * SLC_BLOCK)
        kg = k[bi, gi, tok]
        vg = v[bi, gi, tok]
        s = jnp.einsum('bgnqd,bgqkd->bgnqk', qx, kg) * scale
        qpos = s0 + jnp.arange(SLC_Q_BLOCK)
        mask = (tok <= qpos[:, None])[:, :, None]
        p = masked_softmax(s, mask)
        return jnp.einsum('bgnqk,bgqkd->bgnqd', p.astype(vg.dtype), vg)

    o = lax.map(one, (qb, sb, t0))
    return jnp.moveaxis(o, 0, 3).reshape(B, KVH, G, T, dh)


def setup_inputs(seed: int = 0) -> dict:
    key = jax.random.key(seed)
    ks = jax.random.split(key, 16)
    n = jax.random.normal
    f32 = jnp.float32
    return {
        "x": n(ks[0], (BATCH, SEQ, D_MODEL), f32),
        "p": n(ks[1], (DEPTH, BATCH, SEQ, PLE_DIM), f32),
        "norm_g": 1.0 + 0.05 * n(ks[2], (DEPTH, D_MODEL), f32),
        "w_in": n(ks[3], (DEPTH, D_MODEL, IN_COLS), f32) * D_MODEL ** -0.5,
        "hgrn_lb": 0.5 * n(ks[4], (DEPTH, HG_HEADS * HG_DK), f32),
        "hgrn_onorm_g": 1.0 + 0.05 * n(ks[5], (DEPTH, HG_DV), f32),
        "nsa_qnorm_g": 1.0 + 0.05 * n(ks[6], (DEPTH, NSA_DH), f32),
        "nsa_knorm_g": 1.0 + 0.05 * n(ks[7], (DEPTH, 3, NSA_DH), f32),
        "cmp_pe": 0.1 * n(ks[8], (DEPTH, 2, CMP_BLOCK, NSA_DH), f32),
        "cmp_w1": n(ks[9], (DEPTH, 2, CMP_BLOCK * NSA_DH, CMP_HIDDEN), f32) * (CMP_BLOCK * NSA_DH) ** -0.5,
        "cmp_w2": n(ks[10], (DEPTH, 2, CMP_HIDDEN, NSA_DH), f32) * CMP_HIDDEN ** -0.5,
        "w_out": n(ks[11], (DEPTH, D_MIX, D_MODEL), f32) * (0.5 * D_MIX ** -0.5),
        "ple_norm_g": 1.0 + 0.05 * n(ks[12], (DEPTH, D_MODEL), f32),
        "w_pg": n(ks[13], (DEPTH, D_MODEL, D_MODEL), f32) * D_MODEL ** -0.5,
        "w_pp": n(ks[14], (DEPTH, PLE_DIM, D_MODEL), f32) * (0.5 * PLE_DIM ** -0.5),
    }


def reference(x, p, norm_g, w_in, hgrn_lb, hgrn_onorm_g, nsa_qnorm_g, nsa_knorm_g,
              cmp_pe, cmp_w1, cmp_w2, w_out, ple_norm_g, w_pg, w_pp):
    B, T, _ = x.shape
    KVH, G, dh = NSA_KV_HEADS, NSA_G, NSA_DH
    scale = dh ** -0.5
    split_points = np.cumsum(IN_SPLITS)[:-1].tolist()

    cos, sin = rope_tables(T)
    lb_all = jnp.cumsum(jax.nn.softmax(hgrn_lb.astype(jnp.float32), axis=0), axis=0)
    lb_all = lb_all - lb_all[0]

    NC = (T - CMP_BLOCK) // CMP_STRIDE + 1
    NSB = T // SLC_BLOCK
    top_k = min(SLC_TOPK, NSB)
    c_tok = jnp.arange(NC)[:, None] * CMP_STRIDE + jnp.arange(CMP_BLOCK)[None, :]
    overlap = jnp.mean((c_tok[..., None] // SLC_BLOCK == jnp.arange(NSB)).astype(jnp.float32), axis=1)
    c_end = jnp.arange(NC) * CMP_STRIDE + CMP_BLOCK - 1
    cmp_mask = c_end[None, :] <= jnp.arange(T)[:, None]
    cur = (jnp.arange(T) // SLC_BLOCK)[:, None]
    jb = jnp.arange(NSB)[None, :]
    forced = (jb == 0) | (jb == cur) | (jb == cur - 1)
    eligible = jb <= cur

    h = x
    for i in range(DEPTH):
        xn = rmsnorm(h, norm_g[i])
        proj = xn @ w_in[i]
        (hq, hf, hi, hgo, hz, nq, kcm, vcm, ksl, vsl, kwn, vwn, ngate, nz) = jnp.split(proj, split_points, axis=-1)

        lb = lb_all[i].reshape(HG_HEADS, 1, HG_DK)
        fl = heads(hf, HG_HEADS).astype(jnp.float32)
        log_f = jnp.logaddexp(jnp.log(lb), jnp.log1p(-lb) + jax.nn.log_sigmoid(fl))
        k_hg = (1.0 - lb) * jax.nn.sigmoid(-fl)
        o_hg = hgrn2_chunkwise(heads(hq, HG_HEADS), log_f, k_hg, heads(hi, HG_HEADS))
        o_hg = rmsnorm(o_hg, hgrn_onorm_g[i]).transpose(0, 2, 1, 3).reshape(B, T, HG_W)
        y_hg = o_hg * jax.nn.sigmoid(hgo) * jax.nn.silu(hz)

        qn = rmsnorm(heads(nq, NSA_HEADS), nsa_qnorm_g[i])
        q_nope = qn.reshape(B, KVH, G, T, dh)
        q_rope = partial_rope(qn, cos, sin).reshape(B, KVH, G, T, dh)

        kc = rmsnorm(compress(heads(kcm, KVH), cmp_pe[i, 0], cmp_w1[i, 0], cmp_w2[i, 0]), nsa_knorm_g[i, 0])
        vc = compress(heads(vcm, KVH), cmp_pe[i, 1], cmp_w1[i, 1], cmp_w2[i, 1])
        s_cmp = jnp.einsum('bgntd,bgcd->bgntc', q_nope, kc) * scale
        p_cmp = masked_softmax(s_cmp, cmp_mask)
        o_cmp = jnp.einsum('bgntc,bgcd->bgntd', p_cmp.astype(vc.dtype), vc)

        imp = jnp.einsum('bgntc,cj->bgtj', p_cmp, overlap)
        score = jnp.where(forced, FORCE_SCORE, jnp.where(eligible, imp, -1.0))
        _, sel = lax.top_k(score, top_k)
        k_sl = partial_rope(rmsnorm(heads(ksl, KVH), nsa_knorm_g[i, 1]), cos, sin)
        o_slc = selected_attn(q_rope, k_sl, heads(vsl, KVH), sel, scale)

        k_wn = partial_rope(rmsnorm(heads(kwn, KVH), nsa_knorm_g[i, 2]), cos, sin)
        o_win = window_attn(q_rope, k_wn, heads(vwn, KVH), scale)

        gates = jax.nn.sigmoid(ngate).reshape(B, T, 3, KVH, G).transpose(2, 0, 3, 4, 1)[..., None]
        o_nsa = gates[0] * o_cmp + gates[1] * o_slc + gates[2] * o_win
        o_nsa = o_nsa.reshape(B, NSA_HEADS, T, dh).transpose(0, 2, 1, 3).reshape(B, T, NSA_W)
        y_nsa = o_nsa * jax.nn.silu(nz)

        h = h + jnp.concatenate([y_hg, y_nsa], axis=-1) @ w_out[i]

        gate = jax.nn.sigmoid(rmsnorm(h, ple_norm_g[i]) @ w_pg[i])
        h = h + gate * (p[i] @ w_pp[i])
    return h
```

```python
import functools

import jax
import jax.numpy as jnp
from jax import lax
from jax.experimental import pallas as pl
from jax.experimental.pallas import tpu as pltpu

F32 = jnp.float32
BF16 = jnp.bfloat16

D_MODEL = 1024
HG_HEADS = 4
HG_DK = 128
HG_DV = 128
HG_W = HG_HEADS * HG_DV
HG_CHUNK = 64
HG_SUB = 16
NSA_HEADS = 8
NSA_KV_HEADS = 2
NSA_DH = 64
NSA_G = NSA_HEADS // NSA_KV_HEADS
NSA_W = NSA_HEADS * NSA_DH
KV_W = NSA_KV_HEADS * NSA_DH
CMP_BLOCK = 32
CMP_STRIDE = 16
CMP_HIDDEN = 128
SLC_BLOCK = 64
SLC_TOPK = 16
WINDOW = 512
FORCE_SCORE = 1e4
ROPE_THETA = 500000.0
ROPE_DIM = NSA_DH // 4
PLE_DIM = 256
RMS_EPS = 1e-6
IN_SPLITS = (HG_W,) * 5 + (NSA_W,) + (KV_W,) * 6 + (3 * NSA_HEADS, NSA_W)

LANE = 128
NEG = -1e30
VMEM_LIMIT = 56 * 1024 * 1024

C_HG = 0
C_Q = C_HG + 5 * HG_W
C_KCM = C_Q + NSA_HEADS * LANE
C_VCM = C_KCM + KV_W
C_KSL = C_VCM + KV_W
C_VSL = C_KSL + NSA_KV_HEADS * LANE
C_KWN = C_VSL + NSA_KV_HEADS * LANE
C_VWN = C_KWN + NSA_KV_HEADS * LANE
C_NG = C_VWN + NSA_KV_HEADS * LANE
C_NZ = C_NG + LANE
C_END = C_NZ + NSA_W


def _dot(a, b):
    return jnp.dot(a, b, preferred_element_type=F32)


def _dot_nt(a, b):
    return lax.dot_general(a, b, (((1,), (1,)), ((), ())), preferred_element_type=F32)


def _dot_tn(a, b):
    return lax.dot_general(a, b, (((0,), (0,)), ((), ())), preferred_element_type=F32)


def _sigmoid(x):
    return 1.0 / (1.0 + jnp.exp(-x))


def _silu(x):
    return x * _sigmoid(x)


def _slot_norm(x, g):
    ms = jnp.sum(x * x, axis=-1, keepdims=True) * (1.0 / NSA_DH)
    return x * lax.rsqrt(ms + RMS_EPS) * g


def _slot_rope(x, c, sa, sb):
    return x * c + pltpu.roll(x, LANE - ROPE_DIM // 2, 1) * sa + pltpu.roll(x, ROPE_DIM // 2, 1) * sb


def _proj_kernel(h_ref, g_ref, w_ref, cos_ref, sa_ref, sb_ref, qg_ref, kg_ref,
                 hg_ref, qn_ref, qr_ref, kcm_ref, vcm_ref, ksl_ref, vsl_ref, kwn_ref, vwn_ref,
                 ng_ref, nz_ref):
    x = h_ref[...]
    var = jnp.mean(x * x, axis=-1, keepdims=True)
    xn = (x * lax.rsqrt(var + RMS_EPS) * g_ref[...]).astype(BF16)

    def mm(a, b):
        return _dot(xn, w_ref[:, a:b])

    hg_ref[...] = mm(C_HG, C_Q)
    c, sa, sb = cos_ref[...], sa_ref[...], sb_ref[...]
    scale = NSA_DH ** -0.5
    for hd in range(NSA_HEADS):
        lo = hd * LANE
        qn = _slot_norm(mm(C_Q + lo, C_Q + lo + LANE), qg_ref[...]) * scale
        qn_ref[:, lo:lo + LANE] = qn.astype(BF16)
        qr_ref[:, lo:lo + LANE] = _slot_rope(qn, c, sa, sb).astype(BF16)
    kcm_ref[...] = mm(C_KCM, C_VCM)
    vcm_ref[...] = mm(C_VCM, C_KSL)
    for kv in range(NSA_KV_HEADS):
        lo = kv * LANE
        ks = _slot_norm(mm(C_KSL + lo, C_KSL + lo + LANE), kg_ref[0:1, :])
        ksl_ref[:, lo:lo + LANE] = _slot_rope(ks, c, sa, sb).astype(BF16)
        kw = _slot_norm(mm(C_KWN + lo, C_KWN + lo + LANE), kg_ref[1:2, :])
        kwn_ref[:, lo:lo + LANE] = _slot_rope(kw, c, sa, sb).astype(BF16)
    vsl_ref[...] = mm(C_VSL, C_KWN).astype(BF16)
    vwn_ref[...] = mm(C_VWN, C_NG).astype(BF16)
    ng_ref[...] = mm(C_NG, C_NZ)
    nz_ref[...] = mm(C_NZ, C_END)


def _proj_call(h2, g, w, cos_t, sa_t, sb_t, qg, kg, seq, tm):
    n = h2.shape[0]
    tpb = seq // tm
    row = lambda width: pl.BlockSpec((tm, width), lambda i: (i, 0))
    const = lambda shape: pl.BlockSpec(shape, lambda i: (0, 0))
    tab = pl.BlockSpec((tm, LANE), lambda i: (i % tpb, 0))
    out_w = [(5 * HG_W, F32), (NSA_HEADS * LANE, BF16), (NSA_HEADS * LANE, BF16), (KV_W, F32), (KV_W, F32),
             (NSA_KV_HEADS * LANE, BF16), (NSA_KV_HEADS * LANE, BF16), (NSA_KV_HEADS * LANE, BF16),
             (NSA_KV_HEADS * LANE, BF16), (LANE, F32), (NSA_W, F32)]
    return pl.pallas_call(
        _proj_kernel,
        grid=(n // tm,),
        in_specs=[row(D_MODEL), const((1, D_MODEL)), const((D_MODEL, C_END)), tab, tab, tab,
                  const((1, LANE)), const((2, LANE))],
        out_specs=[row(wd) for wd, _ in out_w],
        out_shape=[jax.ShapeDtypeStruct((n, wd), dt) for wd, dt in out_w],
        compiler_params=pltpu.CompilerParams(dimension_semantics=("parallel",), vmem_limit_bytes=VMEM_LIMIT),
        name="in_proj",
    )(h2, g, w, cos_t, sa_t, sb_t, qg, kg)


def _hgrn_kernel(q_ref, f_ref, v_ref, go_ref, z_ref, loglb_ref, l1mlb_ref, omlb_ref, og_ref,
                 y_ref, st_ref, *, tb):
    C, SUB = HG_CHUNK, HG_SUB

    @pl.when(pl.program_id(2) == 0)
    def _():
        st_ref[...] = jnp.zeros_like(st_ref)

    loglb, l1mlb, omlb, og = loglb_ref[...], l1mlb_ref[...], omlb_ref[...], og_ref[...]
    tril = (lax.broadcasted_iota(jnp.int32, (C, C), 0) >= lax.broadcasted_iota(jnp.int32, (C, C), 1)).astype(F32)
    ones_w = jnp.ones((HG_DK, LANE), BF16)
    lane_sub = lax.broadcasted_iota(jnp.int32, (SUB, LANE), 1)
    row_sub = lax.broadcasted_iota(jnp.int32, (SUB, LANE), 0)
    row_c = lax.broadcasted_iota(jnp.int32, (C, HG_DK), 0)

    def chunk(ci, carry):
        r = pl.multiple_of(ci * C, C)
        q = q_ref[pl.ds(r, C), :]
        fl = f_ref[pl.ds(r, C), :]
        v = v_ref[pl.ds(r, C), :]
        e = jnp.exp(-jnp.abs(fl))
        logsig = jnp.minimum(fl, 0.0) - jnp.log1p(e)
        cc = l1mlb + logsig
        lf = jnp.maximum(loglb, cc) + jnp.log1p(jnp.exp(-jnp.abs(loglb - cc)))
        k = omlb * (jnp.where(fl >= 0.0, e, 1.0) / (1.0 + e))
        b = jnp.dot(tril, lf, preferred_element_type=F32, precision=lax.Precision.HIGHEST)
        v16 = v.astype(BF16)

        a_rows = []
        for blk in range(C // SUB):
            r0 = blk * SUB
            q_i, k_i, b_i = q[r0:r0 + SUB], k[r0:r0 + SUB], b[r0:r0 + SUB]
            pieces = []
            for s in range(SUB):
                dec = jnp.exp(jnp.minimum(b_i - b_i[s:s + 1, :], 0.0))
                pieces.append(jnp.where(row_sub >= s, q_i * dec * k_i[s:s + 1, :], 0.0))
            xs = jnp.concatenate(pieces, axis=0).astype(BF16)
            rs = _dot(xs, ones_w)
            a_i = jnp.zeros((SUB, LANE), F32)
            for s in range(SUB):
                a_i = a_i + jnp.where(lane_sub == r0 + s, rs[s * SUB:(s + 1) * SUB], 0.0)
            if blk > 0:
                ref_b = b[r0 - 1:r0, :]
                q_p = (q_i * jnp.exp(b_i - ref_b)).astype(BF16)
                k_p = jnp.where(row_c < r0, k * jnp.exp(jnp.minimum(ref_b - b, 0.0)), 0.0).astype(BF16)
                a_off = _dot_nt(q_p, k_p)
                a_i = a_i[:, :C] + a_off
            else:
                a_i = a_i[:, :C]
            a_rows.append(a_i)
        a = jnp.concatenate(a_rows, axis=0).astype(BF16)

        st = st_ref[...]
        o = _dot(a, v16) + _dot_nt((q * jnp.exp(b)).astype(BF16), st.astype(BF16))
        b_last = b[C - 1:C, :]
        k_dec = (k * jnp.exp(b_last - b)).astype(BF16)
        st_ref[...] = st * jnp.exp(b_last) + _dot_tn(v16, k_dec)

        ms = jnp.mean(o * o, axis=-1, keepdims=True)
        on = o * lax.rsqrt(ms + RMS_EPS) * og
        y_ref[pl.ds(r, C), :] = on * _sigmoid(go_ref[pl.ds(r, C), :]) * _silu(z_ref[pl.ds(r, C), :])
        return carry

    lax.fori_loop(0, tb // C, chunk, 0)


def _hgrn_call(hg3, loglb, l1mlb, omlb, og, tb):
    bsz, seq, _ = hg3.shape
    piece = lambda kk: pl.BlockSpec((None, tb, HG_DK), lambda b, h, i: (b, i, kk * HG_HEADS + h))
    vec = pl.BlockSpec((1, HG_DK), lambda b, h, i: (0, h))
    return pl.pallas_call(
        functools.partial(_hgrn_kernel, tb=tb),
        grid=(bsz, HG_HEADS, seq // tb),
        in_specs=[piece(0), piece(1), piece(2), piece(3), piece(4), vec, vec, vec,
                  pl.BlockSpec((1, HG_DV), lambda b, h, i: (0, 0))],
        out_specs=pl.BlockSpec((None, tb, HG_DV), lambda b, h, i: (b, i, h)),
        out_shape=jax.ShapeDtypeStruct((bsz, seq, HG_W), F32),
        scratch_shapes=[pltpu.VMEM((HG_DV, HG_DK), F32)],
        compiler_params=pltpu.CompilerParams(dimension_semantics=("parallel", "parallel", "arbitrary"),
                                             vmem_limit_bytes=VMEM_LIMIT),
        name="hgrn2",
    )(hg3, hg3, hg3, hg3, hg3, loglb, l1mlb, omlb, og)


def _cmp_kernel(seg_ref, pe_ref, w1a_ref, w1b_ref, w2_ref, g_ref, out_ref, *, norm):
    seg = seg_ref[...]
    u = _dot((seg + pe_ref[0:1, :]).astype(BF16), w1a_ref[...])
    w = _dot((seg + pe_ref[1:2, :]).astype(BF16), w1b_ref[...])
    hid = u + pltpu.roll(w, w.shape[0] - 1, 0)
    out = _dot(_silu(hid).astype(BF16), w2_ref[...])
    for kv in range(NSA_KV_HEADS):
        o = out[:, kv * LANE:(kv + 1) * LANE]
        if norm:
            o = _slot_norm(o, g_ref[...])
        out_ref[:, kv * LANE:(kv + 1) * LANE] = o.astype(BF16)


def _cmp_call(seg, pe2, w1a, w1b, w2, g, norm):
    bsz, ncp, width = seg.shape
    const = lambda a: pl.BlockSpec(a.shape, lambda b: (0,) * a.ndim)
    return pl.pallas_call(
        functools.partial(_cmp_kernel, norm=norm),
        grid=(bsz,),
        in_specs=[pl.BlockSpec((None, ncp, width), lambda b: (b, 0, 0)), const(pe2), const(w1a), const(w1b),
                  const(w2), const(g)],
        out_specs=pl.BlockSpec((None, ncp, NSA_KV_HEADS * LANE), lambda b: (b, 0, 0)),
        out_shape=jax.ShapeDtypeStruct((bsz, ncp, NSA_KV_HEADS * LANE), BF16),
        compiler_params=pltpu.CompilerParams(dimension_semantics=("parallel",), vmem_limit_bytes=VMEM_LIMIT),
        name="compress_k" if norm else "compress_v",
    )(seg, pe2, w1a, w1b, w2, g)


def _nsa_kernel(qn_ref, qr_ref, kc_ref, vc_ref, ksl_ref, vsl_ref, kwn_ref, vwn_ref, ng_ref, nz_ref,
                ovt_ref, et_ref, y_ref, acc_ref, m_ref, l_ref, *, tq, tk, seq):
    G = NSA_G
    kvh = pl.program_id(1)
    t0 = pl.program_id(2) * tq
    ncp = kc_ref.shape[0]
    nsb = ovt_ref.shape[0]
    topk = min(SLC_TOPK, nsb)

    kc, vc = kc_ref[...], vc_ref[...]
    c_idx = lax.broadcasted_iota(jnp.int32, (tq, ncp), 1)
    t_cmp = t0 + lax.broadcasted_iota(jnp.int32, (tq, ncp), 0)
    cmask = c_idx * CMP_STRIDE + (CMP_BLOCK - 1) <= t_cmp
    p_sum = jnp.zeros((tq, ncp), F32)
    o_cmp = []
    for g in range(G):
        s = _dot_nt(qn_ref[:, g * LANE:(g + 1) * LANE], kc)
        m = jnp.max(jnp.where(cmask, s, NEG), axis=-1, keepdims=True)
        e = jnp.where(cmask, jnp.exp(s - m), 0.0)
        p = e / jnp.maximum(jnp.sum(e, axis=-1, keepdims=True), 1e-30)
        p_sum = p_sum + p
        o_cmp.append(_dot(p.astype(BF16), vc))

    p_hi = p_sum.astype(BF16)
    p_lo = (p_sum - p_hi.astype(F32)).astype(BF16)
    ovt = ovt_ref[...]
    imp_t = _dot_nt(ovt, p_hi) + _dot_nt(ovt, p_lo)
    j_row = lax.broadcasted_iota(jnp.int32, (nsb, tq), 0)
    cur = jnp.right_shift(t0 + lax.broadcasted_iota(jnp.int32, (nsb, tq), 1), SLC_BLOCK.bit_length() - 1)
    forced = (j_row == 0) | (j_row == cur) | (j_row == cur - 1)
    score = jnp.where(forced, FORCE_SCORE, jnp.where(j_row <= cur, imp_t, -1.0))
    beats = jnp.zeros((nsb, tq), F32)
    for jp in range(nsb):
        row = score[jp:jp + 1, :]
        ge = jnp.where(row >= score, 1.0, 0.0)
        gt = jnp.where(row > score, 1.0, 0.0)
        beats = beats + jnp.where(j_row > jp, ge, gt)
    sel_t = jnp.where(beats < topk, 1.0, 0.0)
    if nsb < LANE:
        sel_t = jnp.concatenate([sel_t, jnp.zeros((LANE - nsb, tq), F32)], axis=0)
    sel = sel_t.T.astype(BF16)

    for g in range(G):
        m_ref[g] = jnp.full((tq, 1), NEG, F32)
        l_ref[g] = jnp.zeros((tq, 1), F32)
        acc_ref[g] = jnp.zeros((tq, LANE), F32)
    q_pos = t0 + lax.broadcasted_iota(jnp.int32, (tq, tk), 0)
    k_off = lax.broadcasted_iota(jnp.int32, (tq, tk), 1)

    def slc_tile(j, carry):
        r = pl.multiple_of(j * tk, tk)
        ks, vs = ksl_ref[pl.ds(r, tk), :], vsl_ref[pl.ds(r, tk), :]
        member = _dot_nt(sel, et_ref[pl.ds(r, tk), :])
        mask = (member > 0.5) & (k_off + r <= q_pos)
        for g in range(G):
            s = _dot_nt(qr_ref[:, g * LANE:(g + 1) * LANE], ks)
            m_old = m_ref[g]
            m_new = jnp.maximum(m_old, jnp.max(jnp.where(mask, s, NEG), axis=-1, keepdims=True))
            alpha = jnp.exp(m_old - m_new)
            p = jnp.where(mask, jnp.exp(s - m_new), 0.0)
            l_ref[g] = alpha * l_ref[g] + jnp.sum(p, axis=-1, keepdims=True)
            acc_ref[g] = alpha * acc_ref[g] + _dot(p.astype(BF16), vs)
            m_ref[g] = m_new
        return carry

    lax.fori_loop(0, (t0 + tq + tk - 1) // tk, slc_tile, 0)

    kw_len = WINDOW + tq
    w0 = pl.multiple_of(jnp.maximum(t0 - WINDOW, 0), tq)
    kw, vw = kwn_ref[pl.ds(w0, kw_len), :], vwn_ref[pl.ds(w0, kw_len), :]
    dist = (t0 + lax.broadcasted_iota(jnp.int32, (tq, kw_len), 0)) - (w0 + lax.broadcasted_iota(jnp.int32, (tq, kw_len), 1))
    wmask = (dist >= 0) & (dist < WINDOW)

    sig = _sigmoid(ng_ref[...])
    lane = lax.broadcasted_iota(jnp.int32, (tq, LANE), 1)

    def gate(branch, g):
        col = branch * NSA_HEADS + kvh * G + g
        return jnp.sum(jnp.where(lane == col, sig, 0.0), axis=-1, keepdims=True)

    merged = []
    for g in range(G):
        s = _dot_nt(qr_ref[:, g * LANE:(g + 1) * LANE], kw)
        m = jnp.max(jnp.where(wmask, s, NEG), axis=-1, keepdims=True)
        e = jnp.where(wmask, jnp.exp(s - m), 0.0)
        o_win = _dot(e.astype(BF16), vw) / jnp.sum(e, axis=-1, keepdims=True)
        o_slc = acc_ref[g] / l_ref[g]
        merged.append(gate(0, g) * o_cmp[g] + gate(1, g) * o_slc + gate(2, g) * o_win)
    for pair in range(G // 2):
        o = jnp.where(lane < NSA_DH, merged[2 * pair], merged[2 * pair + 1])
        y_ref[:, pair * LANE:(pair + 1) * LANE] = o * _silu(nz_ref[:, pair * LANE:(pair + 1) * LANE])


def _nsa_call(qn, qr, kc, vc, ksl, vsl, kwn, vwn, ng, nz, ovt, et, tq, tk):
    bsz, seq, _ = qn.shape
    ncp = kc.shape[1]
    qspec = pl.BlockSpec((None, tq, NSA_G * LANE), lambda b, k, i: (b, i, k))
    cspec = pl.BlockSpec((None, ncp, LANE), lambda b, k, i: (b, 0, k))
    kspec = pl.BlockSpec((None, seq, LANE), lambda b, k, i: (b, 0, k))
    const = lambda a: pl.BlockSpec(a.shape, lambda b, k, i: (0, 0))
    yspec = pl.BlockSpec((None, tq, NSA_G * NSA_DH), lambda b, k, i: (b, i, k))
    return pl.pallas_call(
        functools.partial(_nsa_kernel, tq=tq, tk=tk, seq=seq),
        grid=(bsz, NSA_KV_HEADS, seq // tq),
        in_specs=[qspec, qspec, cspec, cspec, kspec, kspec, kspec, kspec,
                  pl.BlockSpec((None, tq, LANE), lambda b, k, i: (b, i, 0)), yspec, const(ovt), const(et)],
        out_specs=yspec,
        out_shape=jax.ShapeDtypeStruct((bsz, seq, NSA_W), F32),
        scratch_shapes=[pltpu.VMEM((NSA_G, tq, LANE), F32), pltpu.VMEM((NSA_G, tq, 1), F32),
                        pltpu.VMEM((NSA_G, tq, 1), F32)],
        compiler_params=pltpu.CompilerParams(dimension_semantics=("parallel", "parallel", "arbitrary"),
                                             vmem_limit_bytes=VMEM_LIMIT),
        name="nsa_attn",
    )(qn, qr, kc, vc, ksl, vsl, kwn, vwn, ng, nz, ovt, et)


def _out_kernel(h_ref, yh_ref, yn_ref, p_ref, wo_ref, g_ref, wpg_ref, wpp_ref, o_ref):
    h = h_ref[...] + _dot(yh_ref[...].astype(BF16), wo_ref[0:HG_W, :]) + _dot(yn_ref[...].astype(BF16), wo_ref[HG_W:, :])
    var = jnp.mean(h * h, axis=-1, keepdims=True)
    hn = (h * lax.rsqrt(var + RMS_EPS) * g_ref[...]).astype(BF16)
    gate = _sigmoid(_dot(hn, wpg_ref[...]))
    o_ref[...] = h + gate * _dot(p_ref[...].astype(BF16), wpp_ref[...])


def _out_call(h2, yh, yn, p2, wo, g, wpg, wpp, tm):
    n = h2.shape[0]
    row = lambda width: pl.BlockSpec((tm, width), lambda i: (i, 0))
    const = lambda a: pl.BlockSpec(a.shape, lambda i: (0, 0))
    return pl.pallas_call(
        _out_kernel,
        grid=(n // tm,),
        in_specs=[row(D_MODEL), row(HG_W), row(NSA_W), row(PLE_DIM), const(wo), const(g), const(wpg), const(wpp)],
        out_specs=row(D_MODEL),
        out_shape=jax.ShapeDtypeStruct((n, D_MODEL), F32),
        compiler_params=pltpu.CompilerParams(dimension_semantics=("parallel",), vmem_limit_bytes=VMEM_LIMIT),
        name="out_proj_ple",
    )(h2, yh, yn, p2, wo, g, wpg, wpp)


def _pad_last(a, width):
    return jnp.pad(a, [(0, 0)] * (a.ndim - 1) + [(0, width - a.shape[-1])])


def _slots(a, nheads, dup=False):
    lead = a.shape[:-1]
    a = a.reshape(lead + (nheads, NSA_DH))
    a = jnp.concatenate([a, a], axis=-1) if dup else _pad_last(a, LANE)
    return a.reshape(lead + (nheads * LANE,))


def _layout_w_in(w_in):
    sp = []
    off = 0
    for wd in IN_SPLITS:
        sp.append(w_in[..., off:off + wd])
        off += wd
    hq, hf, hi, hgo, hz, nq, kcm, vcm, ksl, vsl, kwn, vwn, ngate, nz = sp
    kvh = NSA_KV_HEADS
    cols = [hq, hf, hi, hgo, hz, _slots(nq, NSA_HEADS), kcm, vcm, _slots(ksl, kvh), _slots(vsl, kvh, True),
            _slots(kwn, kvh), _slots(vwn, kvh, True), _pad_last(ngate, LANE), nz]
    return jnp.concatenate(cols, axis=-1).astype(BF16)


def _layout_cmp(pe, w1, w2, dup):
    kvh, half = NSA_KV_HEADS, CMP_BLOCK // 2
    eye = jnp.eye(kvh, dtype=F32)
    w1r = w1.reshape(2, half, NSA_DH, CMP_HIDDEN)
    w1x = jnp.einsum('ardh,gk->argdkh', w1r, eye)
    w1x = w1x.reshape(2, half * kvh * NSA_DH, kvh * CMP_HIDDEN).astype(BF16)
    w2s = jnp.concatenate([w2, w2], axis=-1) if dup else _pad_last(w2, LANE)
    w2x = jnp.einsum('hs,gk->ghks', w2s, eye).reshape(kvh * CMP_HIDDEN, kvh * LANE).astype(BF16)
    pe2 = jnp.broadcast_to(pe.reshape(2, half, 1, NSA_DH), (2, half, kvh, NSA_DH)).reshape(2, half * kvh * NSA_DH)
    return pe2, w1x[0], w1x[1], w2x


def kernel(x, p, norm_g, w_in, hgrn_lb, hgrn_onorm_g, nsa_qnorm_g, nsa_knorm_g, cmp_pe, cmp_w1, cmp_w2, w_out,
           ple_norm_g, w_pg, w_pp):
    bsz, seq, _ = x.shape
    depth = w_in.shape[0]
    n = bsz * seq
    tm = 256
    tq, tk, tb = 128, 512, 512
    ncp = seq // CMP_STRIDE
    nsb = seq // SLC_BLOCK

    pos = jnp.arange(seq, dtype=F32)
    inv = ROPE_THETA ** (-jnp.arange(0, ROPE_DIM, 2, dtype=F32) / ROPE_DIM)
    ang = pos[:, None] * inv[None, :]
    cos, sin = jnp.cos(ang), jnp.sin(ang)
    half = ROPE_DIM // 2
    cos_t = jnp.concatenate([cos, cos, jnp.ones((seq, LANE - ROPE_DIM), F32)], axis=1)
    sa_t = _pad_last(-sin, LANE)
    sb_t = _pad_last(jnp.concatenate([jnp.zeros((seq, half), F32), sin], axis=1), LANE)

    lb_all = jnp.cumsum(jax.nn.softmax(hgrn_lb.astype(F32), axis=0), axis=0)
    lb_all = lb_all - lb_all[0]
    loglb, l1mlb, omlb = jnp.log(lb_all), jnp.log1p(-lb_all), 1.0 - lb_all

    c_tok = jnp.arange(ncp)[:, None] * CMP_STRIDE + jnp.arange(CMP_BLOCK)[None, :]
    overlap = jnp.mean((c_tok[..., None] // SLC_BLOCK == jnp.arange(nsb)).astype(F32), axis=1)
    ovt = overlap.T.astype(BF16)
    et = (jnp.arange(seq)[:, None] // SLC_BLOCK == jnp.arange(LANE)[None, :]).astype(BF16)

    w_all = _layout_w_in(w_in)
    wo16, wpg16, wpp16 = w_out.astype(BF16), w_pg.astype(BF16), w_pp.astype(BF16)
    qg_all = _pad_last(nsa_qnorm_g, LANE)
    kg_all = _pad_last(nsa_knorm_g, LANE)

    h = x.reshape(n, D_MODEL)
    for i in range(depth):
        (hg, qn, qr, kcm, vcm, ksl, vsl, kwn, vwn, ng, nz) = _proj_call(
            h, norm_g[i][None], w_all[i], cos_t, sa_t, sb_t, qg_all[i][None], kg_all[i, 1:3], seq, tm)
        y_hg = _hgrn_call(hg.reshape(bsz, seq, -1), loglb[i][None], l1mlb[i][None], omlb[i][None],
                          hgrn_onorm_g[i][None], tb)
        seg_w = CMP_STRIDE * KV_W
        kc = _cmp_call(kcm.reshape(bsz, ncp, seg_w), *_layout_cmp(cmp_pe[i, 0], cmp_w1[i, 0], cmp_w2[i, 0], False),
                       kg_all[i, 0:1], True)
        vc = _cmp_call(vcm.reshape(bsz, ncp, seg_w), *_layout_cmp(cmp_pe[i, 1], cmp_w1[i, 1], cmp_w2[i, 1], True),
                       kg_all[i, 0:1], False)
        r3 = lambda a: a.reshape(bsz, seq, -1)
        y_nsa = _nsa_call(r3(qn), r3(qr), kc, vc, r3(ksl), r3(vsl), r3(kwn), r3(vwn), r3(ng), r3(nz), ovt, et, tq, tk)
        h = _out_call(h, y_hg.reshape(n, HG_W), y_nsa.reshape(n, NSA_W), p[i].reshape(n, PLE_DIM), wo16[i],
                      ple_norm_g[i][None], wpg16[i], wpp16[i], tm)
    return h.reshape(bsz, seq, D_MODEL)
```

```python
import functools

import jax
import jax.numpy as jnp
from jax import lax
from jax.experimental import pallas as pl
from jax.experimental.pallas import tpu as pltpu

F32 = jnp.float32
BF16 = jnp.bfloat16

D_MODEL = 1024
HG_HEADS = 4
HG_DK = 128
HG_DV = 128
HG_W = HG_HEADS * HG_DV
HG_CHUNK = 64
HG_SUB = 16
NSA_HEADS = 8
NSA_KV_HEADS = 2
NSA_DH = 64
NSA_G = NSA_HEADS // NSA_KV_HEADS
NSA_W = NSA_HEADS * NSA_DH
KV_W = NSA_KV_HEADS * NSA_DH
CMP_BLOCK = 32
CMP_STRIDE = 16
CMP_HIDDEN = 128
SLC_BLOCK = 64
SLC_TOPK = 16
WINDOW = 512
FORCE_SCORE = 1e4
ROPE_THETA = 500000.0
ROPE_DIM = NSA_DH // 4
PLE_DIM = 256
RMS_EPS = 1e-6
IN_SPLITS = (HG_W,) * 5 + (NSA_W,) + (KV_W,) * 6 + (3 * NSA_HEADS, NSA_W)

LANE = 128
NEG = -1e30
LOG2E = 1.4426950408889634
SLC_SHIFT = SLC_BLOCK.bit_length() - 1
ONES_LANE = NSA_DH
PENALTY = -(2.0 ** 100)
VMEM_LIMIT = 56 * 1024 * 1024

C_HG = 0
C_Q = C_HG + 5 * HG_W
C_KCM = C_Q + NSA_HEADS * LANE
C_VCM = C_KCM + KV_W
C_KSL = C_VCM + KV_W
C_VSL = C_KSL + NSA_KV_HEADS * LANE
C_KWN = C_VSL + NSA_KV_HEADS * LANE
C_VWN = C_KWN + NSA_KV_HEADS * LANE
C_NG = C_VWN + NSA_KV_HEADS * LANE
C_NZ = C_NG + LANE
C_END = C_NZ + NSA_W


def _dot(a, b):
    return jnp.dot(a, b, preferred_element_type=F32)


def _dot_nt(a, b):
    return lax.dot_general(a, b, (((1,), (1,)), ((), ())), preferred_element_type=F32)


def _dot_tn(a, b):
    return lax.dot_general(a, b, (((0,), (0,)), ((), ())), preferred_element_type=F32)


def _sigmoid(x):
    return 1.0 / (1.0 + jnp.exp(-x))


def _silu(x):
    return x * _sigmoid(x)


def _slot_norm(x, g):
    ms = jnp.sum(x * x, axis=-1, keepdims=True) * (1.0 / NSA_DH)
    return x * lax.rsqrt(ms + RMS_EPS) * g


def _slot_rope(x, c, sa, sb):
    return x * c + pltpu.roll(x, LANE - ROPE_DIM // 2, 1) * sa + pltpu.roll(x, ROPE_DIM // 2, 1) * sb


def _proj_kernel(h_ref, g_ref, w_ref, cos_ref, sa_ref, sb_ref, qg_ref, kg_ref, lb_ref,
                 hq_ref, lf_ref, lk_ref, hv_ref, hgate_ref, qn_ref, qr_ref, kcm_ref, vcm_ref, ksl_ref, vsl_ref,
                 kwn_ref, vwn_ref, ng_ref, nz_ref, *, tiles_per_seq):
    x = h_ref[...]
    var = jnp.mean(x * x, axis=-1, keepdims=True)
    xn = (x * lax.rsqrt(var + RMS_EPS) * g_ref[...]).astype(BF16)

    def mm(a, b):
        return _dot(xn, w_ref[:, a:b])

    W = HG_W
    hq_ref[...] = mm(C_HG, C_HG + W)
    fl = mm(C_HG + W, C_HG + 2 * W)
    loglb, l1mlb = lb_ref[0:1, :], lb_ref[1:2, :]
    l1pe = jnp.log1p(jnp.exp(-jnp.abs(fl)))
    cc = l1mlb + (jnp.minimum(fl, 0.0) - l1pe)
    b = (jnp.maximum(loglb, cc) + jnp.log1p(jnp.exp(-jnp.abs(loglb - cc)))) * LOG2E
    in_chunk = jnp.bitwise_and(lax.broadcasted_iota(jnp.int32, b.shape, 0), HG_CHUNK - 1)
    step = 1
    while step < HG_CHUNK:
        b = b + jnp.where(in_chunk >= step, pltpu.roll(b, step, 0), 0.0)
        step *= 2
    lf_ref[...] = b
    lk_ref[...] = b - (l1mlb - jnp.maximum(fl, 0.0) - l1pe) * LOG2E
    hv_ref[...] = mm(C_HG + 2 * W, C_HG + 3 * W).astype(BF16)
    hgate_ref[...] = _sigmoid(mm(C_HG + 3 * W, C_HG + 4 * W)) * _silu(mm(C_HG + 4 * W, C_HG + 5 * W))

    c, sa, sb = cos_ref[...], sa_ref[...], sb_ref[...]
    scale = NSA_DH ** -0.5 * LOG2E
    half = NSA_HEADS // 2
    for part in range(2):
        qs = mm(C_Q + part * half * LANE, C_Q + (part + 1) * half * LANE)
        for hd in range(half):
            lo = (part * half + hd) * LANE
            qn = _slot_norm(qs[:, hd * LANE:(hd + 1) * LANE], qg_ref[...]) * scale
            qn_ref[:, lo:lo + LANE] = qn.astype(BF16)
            qr_ref[:, lo:lo + LANE] = _slot_rope(qn, c, sa, sb).astype(BF16)
    kvc = mm(C_KCM, C_KSL)
    kcm_ref[...] = kvc[:, :KV_W]
    vcm_ref[...] = kvc[:, KV_W:]
    tm = x.shape[0]
    lane = lax.broadcasted_iota(jnp.int32, (tm, LANE), 1)
    pos = (pl.program_id(0) % tiles_per_seq) * tm + lax.broadcasted_iota(jnp.int32, (tm, LANE), 0)
    blk_onehot = jnp.where(lane - NSA_DH == jnp.right_shift(pos, SLC_SHIFT), 1.0, 0.0)
    ksl, vsl = mm(C_KSL, C_VSL), mm(C_VSL, C_KWN)
    kwn, vwn = mm(C_KWN, C_VWN), mm(C_VWN, C_NG)
    for kv in range(NSA_KV_HEADS):
        sl = slice(kv * LANE, (kv + 1) * LANE)
        ks = _slot_rope(_slot_norm(ksl[:, sl], kg_ref[0:1, :]), c, sa, sb)
        ksl_ref[:, sl] = jnp.where(lane < NSA_DH, ks, blk_onehot).astype(BF16)
        kwn_ref[:, sl] = _slot_rope(_slot_norm(kwn[:, sl], kg_ref[1:2, :]), c, sa, sb).astype(BF16)
        vsl_ref[:, sl] = jnp.where(lane == ONES_LANE, 1.0, vsl[:, sl]).astype(BF16)
        vwn_ref[:, sl] = jnp.where(lane == ONES_LANE, 1.0, vwn[:, sl]).astype(BF16)
    gz = mm(C_NG, C_END)
    ng_ref[...] = gz[:, :LANE]
    nz_ref[...] = gz[:, LANE:]


def _proj_call(h2, g, w, cos_t, sa_t, sb_t, qg, kg, lbv, seq, tm):
    n = h2.shape[0]
    tpb = seq // tm
    row = lambda width: pl.BlockSpec((tm, width), lambda i: (i, 0))
    const = lambda shape: pl.BlockSpec(shape, lambda i: (0, 0))
    tab = pl.BlockSpec((tm, LANE), lambda i: (i % tpb, 0))
    kv_slots = NSA_KV_HEADS * LANE
    out_w = [(HG_W, F32), (HG_W, F32), (HG_W, F32), (HG_W, BF16), (HG_W, F32),
             (NSA_HEADS * LANE, BF16), (NSA_HEADS * LANE, BF16), (KV_W, F32), (KV_W, F32),
             (kv_slots, BF16), (kv_slots, BF16), (kv_slots, BF16), (kv_slots, BF16), (LANE, F32), (NSA_W, F32)]
    return pl.pallas_call(
        functools.partial(_proj_kernel, tiles_per_seq=tpb),
        grid=(n // tm,),
        in_specs=[row(D_MODEL), const((1, D_MODEL)),
                  pl.BlockSpec((D_MODEL, C_END), lambda i: (0, 0), pipeline_mode=pl.Buffered(1)), tab, tab, tab,
                  const((1, LANE)), const((2, LANE)), const((2, HG_W))],
        out_specs=[row(wd) for wd, _ in out_w],
        out_shape=[jax.ShapeDtypeStruct((n, wd), dt) for wd, dt in out_w],
        compiler_params=pltpu.CompilerParams(dimension_semantics=("parallel",), vmem_limit_bytes=VMEM_LIMIT),
        name="in_proj",
    )(h2, g, w, cos_t, sa_t, sb_t, qg, kg, lbv)


def _hgrn_kernel(q_ref, b_ref, c_ref, v_ref, gate_ref, og_ref, y_ref, st_ref, u_ref, *, tb):
    C, SUB, HALF = HG_CHUNK, HG_SUB, HG_SUB // 2
    nch, nblk = tb // C, C // SUB

    @pl.when(pl.program_id(2) == 0)
    def _():
        st_ref[...] = jnp.zeros_like(st_ref)

    og = og_ref[...]
    lane8 = lax.broadcasted_iota(jnp.int32, (HALF, LANE), 1)
    lane_c = lax.broadcasted_iota(jnp.int32, (SUB, C), 1)
    row_c = lax.broadcasted_iota(jnp.int32, (SUB, C), 0)
    left_col = jnp.where(lax.broadcasted_iota(jnp.int32, (2 * HG_DK, 2 * LANE), 1) < LANE, 1.0, 0.0)
    sum_w = jnp.where(lax.broadcasted_iota(jnp.int32, (2 * HG_DK, 2 * LANE), 0) < HG_DK,
                      left_col, 1.0 - left_col).astype(BF16)

    q = [q_ref[k * C:(k + 1) * C, :] for k in range(nch)]
    b = [b_ref[k * C:(k + 1) * C, :] for k in range(nch)]
    c = [c_ref[k * C:(k + 1) * C, :] for k in range(nch)]
    v16 = [v_ref[k * C:(k + 1) * C, :] for k in range(nch)]

    left, right = [], []
    for k in range(nch):
        for blk in range(nblk):
            r0 = blk * SUB
            q_t, q_b = q[k][r0:r0 + HALF], q[k][r0 + HALF:r0 + SUB]
            b_t, b_b = b[k][r0:r0 + HALF], b[k][r0 + HALF:r0 + SUB]
            tops = [q_t * jnp.exp2(b_t - c[k][r0 + s:r0 + s + 1, :]) for s in range(HALF)]
            bots = [q_b * jnp.exp2(b_b - c[k][r0 + s:r0 + s + 1, :]) for s in range(SUB)]
            left += tops + bots[:HALF // 2]
            right += bots[HALF // 2:]
    per_blk = HALF + HALF // 2
    xs = jnp.concatenate([jnp.concatenate(left, axis=0), jnp.concatenate(right, axis=0)], axis=1)
    rs = _dot(xs.astype(BF16), sum_w)

    a = []
    for k in range(nch):
        a_rows = []
        for blk in range(nblk):
            r0 = blk * SUB
            base = (k * nblk + blk) * per_blk * HALF

            def piece(idx, side):
                return rs[base + idx * HALF:base + (idx + 1) * HALF, side * LANE:(side + 1) * LANE]

            a_t = jnp.zeros((HALF, LANE), F32)
            a_b = jnp.zeros((HALF, LANE), F32)
            for s in range(HALF):
                a_t = jnp.where(lane8 == s, piece(s, 0), a_t)
            for s in range(SUB):
                src = piece(HALF + s, 0) if s < HALF // 2 else piece(s - HALF // 2, 1)
                a_b = jnp.where(lane8 == s, src, a_b)
            a_d = jnp.concatenate([a_t, a_b], axis=0)
            if blk > 0:
                ref_b = b[k][r0 - 1:r0, :]
                q_p = (q[k][r0:r0 + SUB] * jnp.exp2(b[k][r0:r0 + SUB] - ref_b)).astype(BF16)
                k_p = jnp.concatenate([jnp.exp2(ref_b - c[k][:r0]), jnp.zeros((C - r0, HG_DK), F32)], axis=0)
                a_i = pltpu.roll(a_d, r0, 1)[:, :C] + _dot_nt(q_p, k_p.astype(BF16))
            else:
                a_i = a_d[:, :C]
            a_rows.append(jnp.where(lane_c <= r0 + row_c, a_i, 0.0))
        a.append(jnp.concatenate(a_rows, axis=0).astype(BF16))

    for k in range(nch):
        u_ref[k] = _dot_tn(v16[k], jnp.exp2(b[k][C - 1:C, :] - c[k]).astype(BF16))

    st = st_ref[...]
    for k in range(nch):
        o = _dot(a[k], v16[k]) + _dot_nt((q[k] * jnp.exp2(b[k])).astype(BF16), st.astype(BF16))
        st = st * jnp.exp2(b[k][C - 1:C, :]) + u_ref[k]
        ms = jnp.mean(o * o, axis=-1, keepdims=True)
        y_ref[k * C:(k + 1) * C, :] = o * lax.rsqrt(ms + RMS_EPS) * og * gate_ref[k * C:(k + 1) * C, :]
    st_ref[...] = st


def _hgrn_call(hq, lf, lk, hv, hgate, og, tb):
    bsz, seq, _ = hq.shape
    piece = pl.BlockSpec((None, tb, HG_DK), lambda b, h, i: (b, i, h))
    return pl.pallas_call(
        functools.partial(_hgrn_kernel, tb=tb),
        grid=(bsz, HG_HEADS, seq // tb),
        in_specs=[piece, piece, piece, piece, piece, pl.BlockSpec((1, HG_DV), lambda b, h, i: (0, 0))],
        out_specs=piece,
        out_shape=jax.ShapeDtypeStruct((bsz, seq, HG_W), F32),
        scratch_shapes=[pltpu.VMEM((HG_DV, HG_DK), F32), pltpu.VMEM((tb // HG_CHUNK, HG_DV, HG_DK), F32)],
        compiler_params=pltpu.CompilerParams(dimension_semantics=("parallel", "parallel", "arbitrary"),
                                             vmem_limit_bytes=VMEM_LIMIT),
        name="hgrn2",
    )(hq, lf, lk, hv, hgate, og)


def _cmp_kernel(seg_ref, pe_ref, w1a_ref, w1b_ref, w2_ref, g_ref, out_ref, *, norm):
    seg = seg_ref[...]
    u = _dot((seg + pe_ref[0:1, :]).astype(BF16), w1a_ref[...])
    w = _dot((seg + pe_ref[1:2, :]).astype(BF16), w1b_ref[...])
    hid = u + pltpu.roll(w, w.shape[0] - 1, 0)
    out = _dot(_silu(hid).astype(BF16), w2_ref[...])
    for kv in range(NSA_KV_HEADS):
        o = out[:, kv * LANE:(kv + 1) * LANE]
        if norm:
            o = _slot_norm(o, g_ref[...])
        else:
            o = jnp.where(lax.broadcasted_iota(jnp.int32, o.shape, 1) == ONES_LANE, 1.0, o)
        out_ref[:, kv * LANE:(kv + 1) * LANE] = o.astype(BF16)


def _cmp_call(seg, pe2, w1a, w1b, w2, g, norm):
    bsz, ncp, width = seg.shape
    const = lambda a: pl.BlockSpec(a.shape, lambda b: (0,) * a.ndim)
    return pl.pallas_call(
        functools.partial(_cmp_kernel, norm=norm),
        grid=(bsz,),
        in_specs=[pl.BlockSpec((None, ncp, width), lambda b: (b, 0, 0)), const(pe2), const(w1a), const(w1b),
                  const(w2), const(g)],
        out_specs=pl.BlockSpec((None, ncp, NSA_KV_HEADS * LANE), lambda b: (b, 0, 0)),
        out_shape=jax.ShapeDtypeStruct((bsz, ncp, NSA_KV_HEADS * LANE), BF16),
        compiler_params=pltpu.CompilerParams(dimension_semantics=("parallel",), vmem_limit_bytes=VMEM_LIMIT),
        name="compress_k" if norm else "compress_v",
    )(seg, pe2, w1a, w1b, w2, g)


def _attend(q, k, v, bias, acc_ref, m_ref):
    s = _dot_nt(q, k)
    if bias is not None:
        s = s + bias
    m_old = m_ref[...]
    m_new = jnp.maximum(m_old, jnp.max(s, axis=-1, keepdims=True))
    p = jnp.exp2(s - jnp.concatenate([m_new] * (s.shape[1] // LANE), axis=1))
    acc_ref[...] = jnp.exp2(m_old - m_new) * acc_ref[...] + _dot(p.astype(BF16), v)
    m_ref[...] = m_new


def _nsa_kernel(qn_ref, qr_ref, kc_ref, vc_ref, ksl_ref, vsl_ref, kwn_ref, vwn_ref, ng_ref, nz_ref,
                ovt_ref, y_ref, qa_ref, acc_s_ref, m_s_ref, acc_w_ref, *, tq, tk):
    G = NSA_G
    R = G * tq
    kvh = pl.program_id(1)
    i = pl.program_id(2)
    t0 = i * tq
    ncp = kc_ref.shape[0]
    nsb = ovt_ref.shape[0]
    topk = min(SLC_TOPK, nsb)

    def heads_on_rows(ref):
        return jnp.concatenate([ref[:, g * LANE:(g + 1) * LANE] for g in range(G)], axis=0)

    def all_heads(a):
        return jnp.concatenate([a] * G, axis=0)

    row_t = lax.broadcasted_iota(jnp.int32, (tq, 1), 0)

    c_end = lax.broadcasted_iota(jnp.int32, (tq, ncp), 1) * CMP_STRIDE + (CMP_BLOCK - 1)
    s = _dot_nt(heads_on_rows(qn_ref), kc_ref[...]) + all_heads(jnp.where(c_end <= t0 + row_t, 0.0, NEG))
    e = jnp.exp2(s - jnp.max(s, axis=-1, keepdims=True))
    any_valid = all_heads(jnp.where(t0 + row_t >= CMP_BLOCK - 1, 1.0, 0.0))
    p = e * (any_valid / jnp.sum(e, axis=-1, keepdims=True))
    o_cmp = _dot(p.astype(BF16), vc_ref[...])
    p_sum = p[0:tq]
    for g in range(1, G):
        p_sum = p_sum + p[g * tq:(g + 1) * tq]

    p_hi = p_sum.astype(BF16)
    p_lo = (p_sum - p_hi.astype(F32)).astype(BF16)
    ovt = ovt_ref[...]
    imp_t = _dot_nt(ovt, p_hi) + _dot_nt(ovt, p_lo)
    j_row = lax.broadcasted_iota(jnp.int32, (nsb, tq), 0)
    cur = jnp.right_shift(t0 + lax.broadcasted_iota(jnp.int32, (nsb, tq), 1), SLC_SHIFT)
    forced = (j_row == 0) | (j_row == cur) | (j_row == cur - 1)
    score = jnp.where(forced, FORCE_SCORE, jnp.where(j_row <= cur, imp_t, -1.0))
    ngrp = nsb // 8
    grp = [score[8 * a:8 * a + 8] for a in range(ngrp)]
    beats = [jnp.zeros((8, tq), F32) for _ in range(ngrp)]
    sub8 = lax.broadcasted_iota(jnp.int32, (8, tq), 0)
    for jp in range(nsb):
        row = score[jp:jp + 1, :]
        for a in range(ngrp):
            if a > jp // 8:
                hit = jnp.where(row >= grp[a], 1.0, 0.0)
            elif a < jp // 8:
                hit = jnp.where(row > grp[a], 1.0, 0.0)
            else:
                hit = jnp.where(sub8 > jp % 8, jnp.where(row >= grp[a], 1.0, 0.0), jnp.where(row > grp[a], 1.0, 0.0))
            beats[a] = beats[a] + hit
    pen_t = [jnp.where(b < topk, 0.0, PENALTY) for b in beats]
    zeros_t = jnp.zeros((NSA_DH, tq), F32)
    pen_t = jnp.concatenate([zeros_t] + pen_t + ([zeros_t[:LANE - NSA_DH - nsb]] if nsb < LANE - NSA_DH else []), axis=0)
    pen = pen_t.T.astype(BF16)
    lane = lax.broadcasted_iota(jnp.int32, (tq, LANE), 1)
    for g in range(G):
        qa_ref[g * tq:(g + 1) * tq, :] = jnp.where(lane < NSA_DH, qr_ref[:, g * LANE:(g + 1) * LANE], pen)

    def causal_bias(width):
        col = lax.broadcasted_iota(jnp.int32, (tq, width), 1)
        return all_heads(jnp.where(col - (width - tq) <= row_t, 0.0, NEG))

    m_s_ref[...] = jnp.full((R, LANE), NEG, F32)
    acc_s_ref[...] = jnp.zeros((R, LANE), F32)
    qa = qa_ref[...]
    odd = i % 2

    @pl.when(odd == 0)
    def _():
        r = pl.multiple_of(t0, tq)
        _attend(qa, ksl_ref[pl.ds(r, tq), :], vsl_ref[pl.ds(r, tq), :], causal_bias(tq), acc_s_ref, m_s_ref)

    @pl.when(odd == 1)
    def _():
        r = pl.multiple_of(jnp.maximum(t0 - tq, 0), tq)
        _attend(qa, ksl_ref[pl.ds(r, tk), :], vsl_ref[pl.ds(r, tk), :], causal_bias(tk), acc_s_ref, m_s_ref)

    def big_tile(j, carry):
        r = pl.multiple_of(j * tk, tk)
        _attend(qa, ksl_ref[pl.ds(r, tk), :], vsl_ref[pl.ds(r, tk), :], None, acc_s_ref, m_s_ref)
        return carry

    lax.fori_loop(0, i // 2, big_tile, 0)

    q4 = heads_on_rows(qr_ref)

    def window(r, width, bias):
        s = _dot_nt(q4, kwn_ref[pl.ds(r, width), :]) + bias
        p = jnp.exp2(s - jnp.max(s, axis=-1, keepdims=True))
        acc_w_ref[...] = _dot(p.astype(BF16), vwn_ref[pl.ds(r, width), :])

    @pl.when(i == 0)
    def _():
        window(0, tq, causal_bias(tq))

    @pl.when(i == 1)
    def _():
        window(0, 2 * tq, causal_bias(2 * tq))

    @pl.when(i >= 2)
    def _():
        col = lax.broadcasted_iota(jnp.int32, (tq, 3 * tq), 1)
        dist = row_t + 2 * tq - col
        bias = jnp.where(dist >= 0, jnp.where(dist < WINDOW, 0.0, NEG), NEG)
        window(pl.multiple_of(jnp.maximum(t0 - 2 * tq, 0), tq), 3 * tq, all_heads(bias))

    sig = _sigmoid(ng_ref[...])

    def gate(branch, g):
        col = branch * NSA_HEADS + kvh * G + g
        return jnp.sum(jnp.where(lane == col, sig, 0.0), axis=-1, keepdims=True)

    merged = []
    for g in range(G):
        rows = slice(g * tq, (g + 1) * tq)
        o_s, o_w = acc_s_ref[rows, :], acc_w_ref[rows, :]
        merged.append(gate(0, g) * o_cmp[rows] + (gate(1, g) / o_s[:, ONES_LANE:ONES_LANE + 1]) * o_s
                      + (gate(2, g) / o_w[:, ONES_LANE:ONES_LANE + 1]) * o_w)
    for pair in range(G // 2):
        o = jnp.where(lane < NSA_DH, merged[2 * pair], pltpu.roll(merged[2 * pair + 1], NSA_DH, 1))
        y_ref[:, pair * LANE:(pair + 1) * LANE] = o * _silu(nz_ref[:, pair * LANE:(pair + 1) * LANE])


def _nsa_call(qn, qr, kc, vc, ksl, vsl, kwn, vwn, ng, nz, ovt, tq, tk):
    bsz, seq, _ = qn.shape
    ncp = kc.shape[1]
    rows = NSA_G * tq
    qspec = pl.BlockSpec((None, tq, NSA_G * LANE), lambda b, k, i: (b, i, k))
    cspec = pl.BlockSpec((None, ncp, LANE), lambda b, k, i: (b, 0, k))
    kspec = pl.BlockSpec((None, seq, LANE), lambda b, k, i: (b, 0, k))
    yspec = pl.BlockSpec((None, tq, NSA_G * NSA_DH), lambda b, k, i: (b, i, k))
    return pl.pallas_call(
        functools.partial(_nsa_kernel, tq=tq, tk=tk),
        grid=(bsz, NSA_KV_HEADS, seq // tq),
        in_specs=[qspec, qspec, cspec, cspec, kspec, kspec, kspec, kspec,
                  pl.BlockSpec((None, tq, LANE), lambda b, k, i: (b, i, 0)), yspec,
                  pl.BlockSpec(ovt.shape, lambda b, k, i: (0, 0))],
        out_specs=yspec,
        out_shape=jax.ShapeDtypeStruct((bsz, seq, NSA_W), F32),
        scratch_shapes=[pltpu.VMEM((rows, LANE), BF16), pltpu.VMEM((rows, LANE), F32), pltpu.VMEM((rows, LANE), F32),
                        pltpu.VMEM((rows, LANE), F32)],
        compiler_params=pltpu.CompilerParams(dimension_semantics=("parallel", "parallel", "arbitrary"),
                                             vmem_limit_bytes=VMEM_LIMIT),
        name="nsa_attn",
    )(qn, qr, kc, vc, ksl, vsl, kwn, vwn, ng, nz, ovt)


def _out_kernel(h_ref, yh_ref, yn_ref, p_ref, wo_ref, g_ref, wpg_ref, wpp_ref, o_ref):
    h = h_ref[...] + _dot(yh_ref[...].astype(BF16), wo_ref[0:HG_W, :]) + _dot(yn_ref[...].astype(BF16), wo_ref[HG_W:, :])
    var = jnp.mean(h * h, axis=-1, keepdims=True)
    hn = (h * lax.rsqrt(var + RMS_EPS) * g_ref[...]).astype(BF16)
    gate = _sigmoid(_dot(hn, wpg_ref[...]))
    o_ref[...] = h + gate * _dot(p_ref[...].astype(BF16), wpp_ref[...])


def _out_call(h2, yh, yn, p2, wo, g, wpg, wpp, tm):
    n = h2.shape[0]
    row = lambda width: pl.BlockSpec((tm, width), lambda i: (i, 0))
    const = lambda a: pl.BlockSpec(a.shape, lambda i: (0, 0))
    return pl.pallas_call(
        _out_kernel,
        grid=(n // tm,),
        in_specs=[row(D_MODEL), row(HG_W), row(NSA_W), row(PLE_DIM), const(wo), const(g), const(wpg), const(wpp)],
        out_specs=row(D_MODEL),
        out_shape=jax.ShapeDtypeStruct((n, D_MODEL), F32),
        compiler_params=pltpu.CompilerParams(dimension_semantics=("parallel",), vmem_limit_bytes=VMEM_LIMIT),
        name="out_proj_ple",
    )(h2, yh, yn, p2, wo, g, wpg, wpp)


def _pad_last(a, width):
    return jnp.pad(a, [(0, 0)] * (a.ndim - 1) + [(0, width - a.shape[-1])])


def _slots(a, nheads):
    lead = a.shape[:-1]
    a = _pad_last(a.reshape(lead + (nheads, NSA_DH)), LANE)
    return a.reshape(lead + (nheads * LANE,))


def _layout_w_in(w_in):
    sp = []
    off = 0
    for wd in IN_SPLITS:
        sp.append(w_in[..., off:off + wd])
        off += wd
    hq, hf, hi, hgo, hz, nq, kcm, vcm, ksl, vsl, kwn, vwn, ngate, nz = sp
    kvh = NSA_KV_HEADS
    cols = [hq, hf, hi, hgo, hz, _slots(nq, NSA_HEADS), kcm, vcm, _slots(ksl, kvh), _slots(vsl, kvh),
            _slots(kwn, kvh), _slots(vwn, kvh), _pad_last(ngate, LANE), nz]
    return jnp.concatenate(cols, axis=-1).astype(BF16)


def _layout_cmp(pe, w1, w2):
    kvh, half = NSA_KV_HEADS, CMP_BLOCK // 2
    eye = jnp.eye(kvh, dtype=F32)
    w1r = w1.reshape(2, half, NSA_DH, CMP_HIDDEN)
    w1x = jnp.einsum('ardh,gk->argdkh', w1r, eye)
    w1x = w1x.reshape(2, half * kvh * NSA_DH, kvh * CMP_HIDDEN).astype(BF16)
    w2s = _pad_last(w2, LANE)
    w2x = jnp.einsum('hs,gk->ghks', w2s, eye).reshape(kvh * CMP_HIDDEN, kvh * LANE).astype(BF16)
    pe2 = jnp.broadcast_to(pe.reshape(2, half, 1, NSA_DH), (2, half, kvh, NSA_DH)).reshape(2, half * kvh * NSA_DH)
    return pe2, w1x[0], w1x[1], w2x


def kernel(x, p, norm_g, w_in, hgrn_lb, hgrn_onorm_g, nsa_qnorm_g, nsa_knorm_g, cmp_pe, cmp_w1, cmp_w2, w_out,
           ple_norm_g, w_pg, w_pp):
    bsz, seq, _ = x.shape
    depth = w_in.shape[0]
    n = bsz * seq
    tm = 256
    tq, tk, tb = WINDOW // 2, WINDOW, 512
    assert seq % tk == 0 and seq // SLC_BLOCK <= LANE - NSA_DH and (seq // SLC_BLOCK) % 8 == 0
    ncp = seq // CMP_STRIDE
    nsb = seq // SLC_BLOCK

    pos = jnp.arange(seq, dtype=F32)
    inv = ROPE_THETA ** (-jnp.arange(0, ROPE_DIM, 2, dtype=F32) / ROPE_DIM)
    ang = pos[:, None] * inv[None, :]
    cos, sin = jnp.cos(ang), jnp.sin(ang)
    half = ROPE_DIM // 2
    cos_t = jnp.concatenate([cos, cos, jnp.ones((seq, LANE - ROPE_DIM), F32)], axis=1)
    sa_t = _pad_last(-sin, LANE)
    sb_t = _pad_last(jnp.concatenate([jnp.zeros((seq, half), F32), sin], axis=1), LANE)

    lb_all = jnp.cumsum(jax.nn.softmax(hgrn_lb.astype(F32), axis=0), axis=0)
    lb_all = lb_all - lb_all[0]
    lbv = jnp.stack([jnp.log(lb_all), jnp.log1p(-lb_all)], axis=1)

    c_tok = jnp.arange(ncp)[:, None] * CMP_STRIDE + jnp.arange(CMP_BLOCK)[None, :]
    overlap = jnp.mean((c_tok[..., None] // SLC_BLOCK == jnp.arange(nsb)).astype(F32), axis=1)
    ovt = overlap.T.astype(BF16)

    w_all = _layout_w_in(w_in)
    wo16, wpg16, wpp16 = w_out.astype(BF16), w_pg.astype(BF16), w_pp.astype(BF16)
    qg_all = _pad_last(nsa_qnorm_g, LANE)
    kg_all = _pad_last(nsa_knorm_g, LANE)

    h = x.reshape(n, D_MODEL)
    for i in range(depth):
        r3 = lambda a: a.reshape(bsz, seq, -1)
        (hq, lf, lk, hv, hgate, qn, qr, kcm, vcm, ksl, vsl, kwn, vwn, ng, nz) = _proj_call(
            h, norm_g[i][None], w_all[i], cos_t, sa_t, sb_t, qg_all[i][None], kg_all[i, 1:3], lbv[i], seq, tm)
        y_hg = _hgrn_call(r3(hq), r3(lf), r3(lk), r3(hv), r3(hgate), hgrn_onorm_g[i][None], tb)
        seg_w = CMP_STRIDE * KV_W
        kc = _cmp_call(kcm.reshape(bsz, ncp, seg_w), *_layout_cmp(cmp_pe[i, 0], cmp_w1[i, 0], cmp_w2[i, 0]),
                       kg_all[i, 0:1], True)
        vc = _cmp_call(vcm.reshape(bsz, ncp, seg_w), *_layout_cmp(cmp_pe[i, 1], cmp_w1[i, 1], cmp_w2[i, 1]),
                       kg_all[i, 0:1], False)
        y_nsa = _nsa_call(r3(qn), r3(qr), kc, vc, r3(ksl), r3(vsl), r3(kwn), r3(vwn), r3(ng), r3(nz), ovt, tq, tk)
        h = _out_call(h, y_hg.reshape(n, HG_W), y_nsa.reshape(n, NSA_W), p[i].reshape(n, PLE_DIM), wo16[i],
                      ple_norm_g[i][None], wpg16[i], wpp16[i], tm)
    return h.reshape(bsz, seq, D_MODEL)
```

```python
import functools

import jax
import jax.numpy as jnp
from jax import lax
from jax.experimental import pallas as pl
from jax.experimental.pallas import tpu as pltpu

F32 = jnp.float32
BF16 = jnp.bfloat16

D_MODEL = 1024
HG_HEADS = 4
HG_DK = 128
HG_DV = 128
HG_W = HG_HEADS * HG_DV
HG_CHUNK = 64
HG_SUB = 16
NSA_HEADS = 8
NSA_KV_HEADS = 2
NSA_DH = 64
NSA_G = NSA_HEADS // NSA_KV_HEADS
NSA_W = NSA_HEADS * NSA_DH
KV_W = NSA_KV_HEADS * NSA_DH
CMP_BLOCK = 32
CMP_STRIDE = 16
CMP_HIDDEN = 128
SLC_BLOCK = 64
SLC_TOPK = 16
WINDOW = 512
FORCE_SCORE = 1e4
ROPE_THETA = 500000.0
ROPE_DIM = NSA_DH // 4
PLE_DIM = 256
RMS_EPS = 1e-6
IN_SPLITS = (HG_W,) * 5 + (NSA_W,) + (KV_W,) * 6 + (3 * NSA_HEADS, NSA_W)

LANE = 128
NEG = -1e30
LOG2E = 1.4426950408889634
SLC_SHIFT = SLC_BLOCK.bit_length() - 1
MXU_COLS = 256
PENALTY = -(2.0 ** 100)
VMEM_LIMIT = 56 * 1024 * 1024

V_SLOT = LANE

C_HG = 0
C_Q = C_HG + 5 * HG_W
C_KCM = C_Q + NSA_W
C_KSL = C_KCM + 2 * KV_W
C_KWN = C_KSL + 2 * KV_W
C_NG = C_KWN + 2 * KV_W
C_NZ = C_NG + LANE
C_END = C_NZ + NSA_W


def _dot(a, b):
    return jnp.dot(a, b, preferred_element_type=F32)


def _dot_nt(a, b):
    return lax.dot_general(a, b, (((1,), (1,)), ((), ())), preferred_element_type=F32)


def _dot_tn(a, b):
    return lax.dot_general(a, b, (((0,), (0,)), ((), ())), preferred_element_type=F32)


def _sigmoid(x):
    return 1.0 / (1.0 + jnp.exp(-x))


def _silu(x):
    return x * _sigmoid(x)


def _slot_norm(x, g):
    ms = jnp.sum(x * x, axis=-1, keepdims=True) * (1.0 / NSA_DH)
    return x * lax.rsqrt(ms + RMS_EPS) * g


def _proj_kernel(h_ref, g_ref, w_ref, cos_ref, sa_ref, sb_ref, qg_ref, kg_ref, lb_ref,
                 hq_ref, lf_ref, lk_ref, hv_ref, hgate_ref, qn_ref, qr_ref, kcm_ref, vcm_ref, ksl_ref, vsl_ref,
                 kwn_ref, vwn_ref, ng_ref, nz_ref, *, tiles_per_seq):
    x = h_ref[...]
    var = jnp.mean(x * x, axis=-1, keepdims=True)
    xn = (x * lax.rsqrt(var + RMS_EPS) * g_ref[...]).astype(BF16)

    def mm(a, b):
        return _dot(xn, w_ref[:, a:b])

    W = HG_W
    hq_ref[...] = mm(C_HG, C_HG + W)
    fl = mm(C_HG + W, C_HG + 2 * W)
    loglb, l1mlb = lb_ref[0:1, :], lb_ref[1:2, :]
    l1pe = jnp.log1p(jnp.exp(-jnp.abs(fl)))
    cc = l1mlb + (jnp.minimum(fl, 0.0) - l1pe)
    b = (jnp.maximum(loglb, cc) + jnp.log1p(jnp.exp(-jnp.abs(loglb - cc)))) * LOG2E
    in_chunk = jnp.bitwise_and(lax.broadcasted_iota(jnp.int32, b.shape, 0), HG_CHUNK - 1)
    step = 1
    while step < HG_CHUNK:
        b = b + jnp.where(in_chunk >= step, pltpu.roll(b, step, 0), 0.0)
        step *= 2
    lf_ref[...] = b
    lk_ref[...] = b - (l1mlb - jnp.maximum(fl, 0.0) - l1pe) * LOG2E
    hv_ref[...] = mm(C_HG + 2 * W, C_HG + 3 * W).astype(BF16)
    hgate_ref[...] = _sigmoid(mm(C_HG + 3 * W, C_HG + 4 * W)) * _silu(mm(C_HG + 4 * W, C_HG + 5 * W))

    c, sa, sb = cos_ref[...], sa_ref[...], sb_ref[...]
    scale = NSA_DH ** -0.5 * LOG2E
    tm = x.shape[0]
    lane = lax.broadcasted_iota(jnp.int32, (tm, LANE), 1)
    low = lane < NSA_DH

    qs = mm(C_Q, C_KCM)
    kvc = mm(C_KCM, C_KSL)
    kv_s, kv_w = mm(C_KSL, C_KWN), mm(C_KWN, C_NG)
    gz = mm(C_NG, C_END)
    kcm_ref[...] = kvc[:, :KV_W]
    vcm_ref[...] = kvc[:, KV_W:]
    ng_ref[...] = gz[:, :LANE]
    nz_ref[...] = gz[:, LANE:]

    nq = NSA_HEADS // 2
    tiles = [qs[:, pr * LANE:(pr + 1) * LANE] for pr in range(nq)] + [kv_s[:, :KV_W], kv_w[:, :KV_W]]
    gains = [qg_ref[...] * scale] * nq + [kg_ref[0:1, :], kg_ref[1:2, :]]
    half = len(tiles) // 2
    sq = jnp.concatenate([jnp.concatenate([t * t for t in tiles[:half]], axis=0),
                          jnp.concatenate([t * t for t in tiles[half:]], axis=0)], axis=1)
    same_head = (jnp.right_shift(lax.broadcasted_iota(jnp.int32, (2 * LANE, 2 * LANE), 0), NSA_DH.bit_length() - 1)
                 == jnp.right_shift(lax.broadcasted_iota(jnp.int32, (2 * LANE, 2 * LANE), 1), NSA_DH.bit_length() - 1))
    head_sum = jnp.where(same_head, 1.0, 0.0).astype(BF16)
    sq_hi = sq.astype(BF16)
    sq_lo = (sq - sq_hi.astype(F32)).astype(BF16)
    ssq = _dot(sq_hi, head_sum) + _dot(sq_lo, head_sum)
    ms = [ssq[(j % half) * tm:(j % half + 1) * tm, (j // half) * LANE:(j // half + 1) * LANE] * (1.0 / NSA_DH)
          for j in range(len(tiles))]
    normed = [t * lax.rsqrt(m + RMS_EPS) * g for t, m, g in zip(tiles, ms, gains)]
    roped = [t * c + pltpu.roll(t, LANE - ROPE_DIM // 2, 1) * sa + pltpu.roll(t, ROPE_DIM // 2, 1) * sb for t in normed]

    def head_slots(tile):
        return jnp.where(low, tile, 0.0), jnp.where(low, pltpu.roll(tile, NSA_DH, 1), 0.0)

    pos = (pl.program_id(0) % tiles_per_seq) * tm + lax.broadcasted_iota(jnp.int32, (tm, LANE), 0)
    blk_onehot = jnp.where(lane - NSA_DH == jnp.right_shift(pos, SLC_SHIFT), 1.0, 0.0)
    for pr in range(nq):
        for k, (a, b) in enumerate(zip(head_slots(normed[pr]), head_slots(roped[pr]))):
            sl = slice((2 * pr + k) * LANE, (2 * pr + k + 1) * LANE)
            qn_ref[:, sl] = a.astype(BF16)
            qr_ref[:, sl] = b.astype(BF16)
    for kv, (ks, kw) in enumerate(zip(head_slots(roped[nq]), head_slots(roped[nq + 1]))):
        sl = slice(kv * LANE, (kv + 1) * LANE)
        ksl_ref[:, sl] = jnp.where(low, ks, blk_onehot).astype(BF16)
        kwn_ref[:, sl] = kw.astype(BF16)
    for ref, t in ((vsl_ref, kv_s[:, KV_W:]), (vwn_ref, kv_w[:, KV_W:])):
        for kv, v in enumerate((t, pltpu.roll(t, NSA_DH, 1))):
            ref[:, kv * V_SLOT:(kv + 1) * V_SLOT] = jnp.where(low, v, 1.0).astype(BF16)


def _proj_call(h2, g, w, cos_t, sa_t, sb_t, qg, kg, lbv, seq, tm):
    n = h2.shape[0]
    tpb = seq // tm
    row = lambda width: pl.BlockSpec((tm, width), lambda i: (i, 0))
    const = lambda shape: pl.BlockSpec(shape, lambda i: (0, 0))
    tab = pl.BlockSpec((tm, LANE), lambda i: (i % tpb, 0))
    k_slots, v_slots = NSA_KV_HEADS * LANE, NSA_KV_HEADS * V_SLOT
    out_w = [(HG_W, F32), (HG_W, F32), (HG_W, F32), (HG_W, BF16), (HG_W, F32),
             (NSA_HEADS * LANE, BF16), (NSA_HEADS * LANE, BF16), (KV_W, F32), (KV_W, F32),
             (k_slots, BF16), (v_slots, BF16), (k_slots, BF16), (v_slots, BF16), (LANE, F32), (NSA_W, F32)]
    return pl.pallas_call(
        functools.partial(_proj_kernel, tiles_per_seq=tpb),
        grid=(n // tm,),
        in_specs=[row(D_MODEL), const((1, D_MODEL)),
                  pl.BlockSpec((D_MODEL, C_END), lambda i: (0, 0), pipeline_mode=pl.Buffered(1)), tab, tab, tab,
                  const((1, LANE)), const((2, LANE)), const((2, HG_W))],
        out_specs=[row(wd) for wd, _ in out_w],
        out_shape=[jax.ShapeDtypeStruct((n, wd), dt) for wd, dt in out_w],
        compiler_params=pltpu.CompilerParams(dimension_semantics=("parallel",), vmem_limit_bytes=VMEM_LIMIT),
        name="in_proj",
    )(h2, g, w, cos_t, sa_t, sb_t, qg, kg, lbv)


def _hgrn_kernel(q_ref, b_ref, c_ref, v_ref, gate_ref, og_ref, y_ref, st_ref, u_ref, *, tb):
    C, SUB, HALF = HG_CHUNK, HG_SUB, HG_SUB // 2
    nch, nblk = tb // C, C // SUB

    @pl.when(pl.program_id(2) == 0)
    def _():
        st_ref[...] = jnp.zeros_like(st_ref)

    og = og_ref[...]
    lane8 = lax.broadcasted_iota(jnp.int32, (HALF, LANE), 1)
    lane_c = lax.broadcasted_iota(jnp.int32, (SUB, C), 1)
    row_c = lax.broadcasted_iota(jnp.int32, (SUB, C), 0)
    left_col = jnp.where(lax.broadcasted_iota(jnp.int32, (2 * HG_DK, 2 * LANE), 1) < LANE, 1.0, 0.0)
    sum_w = jnp.where(lax.broadcasted_iota(jnp.int32, (2 * HG_DK, 2 * LANE), 0) < HG_DK,
                      left_col, 1.0 - left_col).astype(BF16)

    q = [q_ref[k * C:(k + 1) * C, :] for k in range(nch)]
    b = [b_ref[k * C:(k + 1) * C, :] for k in range(nch)]
    c = [c_ref[k * C:(k + 1) * C, :] for k in range(nch)]
    v16 = [v_ref[k * C:(k + 1) * C, :] for k in range(nch)]

    left, right = [], []
    for k in range(nch):
        for blk in range(nblk):
            r0 = blk * SUB
            q_t, q_b = q[k][r0:r0 + HALF], q[k][r0 + HALF:r0 + SUB]
            b_t, b_b = b[k][r0:r0 + HALF], b[k][r0 + HALF:r0 + SUB]
            tops = [q_t * jnp.exp2(b_t - c[k][r0 + s:r0 + s + 1, :]) for s in range(HALF)]
            bots = [q_b * jnp.exp2(b_b - c[k][r0 + s:r0 + s + 1, :]) for s in range(SUB)]
            left += tops + bots[:HALF // 2]
            right += bots[HALF // 2:]
    per_blk = HALF + HALF // 2
    xs = jnp.concatenate([jnp.concatenate(left, axis=0), jnp.concatenate(right, axis=0)], axis=1)
    rs = _dot(xs.astype(BF16), sum_w)

    a = []
    for k in range(nch):
        a_rows = []
        for blk in range(nblk):
            r0 = blk * SUB
            base = (k * nblk + blk) * per_blk * HALF

            def piece(idx, side):
                return rs[base + idx * HALF:base + (idx + 1) * HALF, side * LANE:(side + 1) * LANE]

            a_t = jnp.zeros((HALF, LANE), F32)
            a_b = jnp.zeros((HALF, LANE), F32)
            for s in range(HALF):
                a_t = jnp.where(lane8 == s, piece(s, 0), a_t)
            for s in range(SUB):
                src = piece(HALF + s, 0) if s < HALF // 2 else piece(s - HALF // 2, 1)
                a_b = jnp.where(lane8 == s, src, a_b)
            a_d = jnp.concatenate([a_t, a_b], axis=0)
            if blk > 0:
                ref_b = b[k][r0 - 1:r0, :]
                q_p = (q[k][r0:r0 + SUB] * jnp.exp2(b[k][r0:r0 + SUB] - ref_b)).astype(BF16)
                k_p = jnp.concatenate([jnp.exp2(ref_b - c[k][:r0]), jnp.zeros((C - r0, HG_DK), F32)], axis=0)
                a_i = pltpu.roll(a_d, r0, 1)[:, :C] + _dot_nt(q_p, k_p.astype(BF16))
            else:
                a_i = a_d[:, :C]
            a_rows.append(jnp.where(lane_c <= r0 + row_c, a_i, 0.0))
        a.append(jnp.concatenate(a_rows, axis=0).astype(BF16))

    for k in range(nch):
        u_ref[k] = _dot_tn(v16[k], jnp.exp2(b[k][C - 1:C, :] - c[k]).astype(BF16))

    st = st_ref[...]
    for k in range(nch):
        o = _dot(a[k], v16[k]) + _dot_nt((q[k] * jnp.exp2(b[k])).astype(BF16), st.astype(BF16))
        st = st * jnp.exp2(b[k][C - 1:C, :]) + u_ref[k]
        ms = jnp.mean(o * o, axis=-1, keepdims=True)
        y_ref[k * C:(k + 1) * C, :] = o * lax.rsqrt(ms + RMS_EPS) * og * gate_ref[k * C:(k + 1) * C, :]
    st_ref[...] = st


def _hgrn_call(hq, lf, lk, hv, hgate, og, tb):
    bsz, seq, _ = hq.shape
    piece = pl.BlockSpec((None, tb, HG_DK), lambda b, h, i: (b, i, h))
    return pl.pallas_call(
        functools.partial(_hgrn_kernel, tb=tb),
        grid=(bsz, HG_HEADS, seq // tb),
        in_specs=[piece, piece, piece, piece, piece, pl.BlockSpec((1, HG_DV), lambda b, h, i: (0, 0))],
        out_specs=piece,
        out_shape=jax.ShapeDtypeStruct((bsz, seq, HG_W), F32),
        scratch_shapes=[pltpu.VMEM((HG_DV, HG_DK), F32), pltpu.VMEM((tb // HG_CHUNK, HG_DV, HG_DK), F32)],
        compiler_params=pltpu.CompilerParams(dimension_semantics=("parallel", "parallel", "arbitrary"),
                                             vmem_limit_bytes=VMEM_LIMIT),
        name="hgrn2",
    )(hq, lf, lk, hv, hgate, og)


def _cmp_kernel(seg_ref, pe_ref, w1a_ref, w1b_ref, w2_ref, g_ref, out_ref, *, norm):
    seg = seg_ref[...]
    u = _dot((seg + pe_ref[0:1, :]).astype(BF16), w1a_ref[...])
    w = _dot((seg + pe_ref[1:2, :]).astype(BF16), w1b_ref[...])
    hid = u + pltpu.roll(w, w.shape[0] - 1, 0)
    out = _dot(_silu(hid).astype(BF16), w2_ref[...])
    if norm:
        for kv in range(NSA_KV_HEADS):
            out_ref[:, kv * LANE:(kv + 1) * LANE] = _slot_norm(out[:, kv * LANE:(kv + 1) * LANE], g_ref[...]).astype(BF16)
    else:
        head_lane = jnp.bitwise_and(lax.broadcasted_iota(jnp.int32, out.shape, 1), LANE - 1)
        out_ref[...] = jnp.where(head_lane < NSA_DH, out, 1.0).astype(BF16)


def _cmp_call(seg, pe2, w1a, w1b, w2, g, norm):
    bsz, ncp, width = seg.shape
    out_w = w2.shape[1]
    const = lambda a: pl.BlockSpec(a.shape, lambda b: (0,) * a.ndim)
    return pl.pallas_call(
        functools.partial(_cmp_kernel, norm=norm),
        grid=(bsz,),
        in_specs=[pl.BlockSpec((None, ncp, width), lambda b: (b, 0, 0)), const(pe2), const(w1a), const(w1b),
                  const(w2), const(g)],
        out_specs=pl.BlockSpec((None, ncp, out_w), lambda b: (b, 0, 0)),
        out_shape=jax.ShapeDtypeStruct((bsz, ncp, out_w), BF16),
        compiler_params=pltpu.CompilerParams(dimension_semantics=("parallel",), vmem_limit_bytes=VMEM_LIMIT),
        name="compress_k" if norm else "compress_v",
    )(seg, pe2, w1a, w1b, w2, g)


def _attend(q, k, v, bias, acc_ref, m_ref):
    p, m_new = _softmax_tiles(q, k, bias, m_ref[...])
    acc_ref[...] = jnp.exp2(m_ref[...] - m_new) * acc_ref[...] + _pv_tiles(p, v)
    m_ref[...] = m_new


def _softmax_tiles(q, k, bias, m_floor):
    pieces = []
    for c in range(k.shape[0] // MXU_COLS):
        s = _dot_nt(q, k[c * MXU_COLS:(c + 1) * MXU_COLS])
        pieces.append(s if bias is None else s + bias[:, c * MXU_COLS:(c + 1) * MXU_COLS])
    row_max = functools.reduce(jnp.maximum, [jnp.max(s, axis=-1, keepdims=True) for s in pieces])
    m = row_max if m_floor is None else jnp.maximum(m_floor, row_max)
    m_wide = jnp.concatenate([jnp.broadcast_to(m, (q.shape[0], LANE))] * (MXU_COLS // LANE), axis=1)
    return [jnp.exp2(s - m_wide).astype(BF16) for s in pieces], m


def _pv_tiles(p, v):
    return functools.reduce(lambda a, b: a + b,
                            [_dot(pc, v[c * MXU_COLS:(c + 1) * MXU_COLS]) for c, pc in enumerate(p)])


def _nsa_kernel(qn_ref, qr_ref, kc_ref, vc_ref, ksl_ref, vsl_ref, kwn_ref, vwn_ref, ng_ref, nz_ref,
                ovt_ref, gexp_ref, y_ref, qa_ref, acc_s_ref, m_s_ref, acc_w_ref, *, tq, tk):
    G = NSA_G
    R = G * tq
    i = pl.program_id(2)
    t0 = i * tq
    ncp = kc_ref.shape[0]
    nsb = ovt_ref.shape[0] - 8
    topk = min(SLC_TOPK, nsb)

    def heads_on_rows(ref):
        return jnp.concatenate([ref[:, g * LANE:(g + 1) * LANE] for g in range(G)], axis=0)

    def all_heads(a):
        return jnp.concatenate([a] * G, axis=0)

    q_pos = t0 + lax.broadcasted_iota(jnp.int32, (tq, 1), 0)

    def position_bias(start, width, window=None):
        dist = q_pos - (start + lax.broadcasted_iota(jnp.int32, (tq, width), 1))
        ok = jnp.where(dist >= 0, 0.0, NEG)
        return all_heads(ok if window is None else jnp.where(dist < window, ok, NEG))

    q4 = heads_on_rows(qr_ref)
    w0 = pl.multiple_of(jnp.maximum(t0 - 2 * tq, 0), tq)
    p_w, _ = _softmax_tiles(q4, kwn_ref[pl.ds(w0, 3 * tq), :], position_bias(w0, 3 * tq, WINDOW), None)
    acc_w_ref[...] = _pv_tiles(p_w, vwn_ref[pl.ds(w0, 3 * tq), :])

    c_end = lax.broadcasted_iota(jnp.int32, (tq, ncp), 1) * CMP_STRIDE + (CMP_BLOCK - 1)
    s = _dot_nt(heads_on_rows(qn_ref), kc_ref[...]) + all_heads(jnp.where(c_end <= q_pos, 0.0, NEG))
    e = jnp.exp2(s - jnp.max(s, axis=-1, keepdims=True))
    e_hi = e.astype(BF16)
    e_lo = (e - e_hi.astype(F32)).astype(BF16)
    o_cmp = _dot(e_hi, vc_ref[...])

    ovt = ovt_ref[...]
    raw_t = _dot_nt(ovt, e_hi) + _dot_nt(ovt, e_lo)
    any_valid_t = jnp.where(t0 + lax.broadcasted_iota(jnp.int32, (1, tq), 1) >= CMP_BLOCK - 1, 1.0, 0.0)
    imp_t = jnp.zeros((nsb, tq), F32)
    for g in range(G):
        imp_t = imp_t + raw_t[:nsb, g * tq:(g + 1) * tq] * (any_valid_t / raw_t[nsb:nsb + 1, g * tq:(g + 1) * tq])
    j_row = lax.broadcasted_iota(jnp.int32, (nsb, tq), 0)
    cur = jnp.right_shift(t0 + lax.broadcasted_iota(jnp.int32, (nsb, tq), 1), SLC_SHIFT)
    forced = (j_row == 0) | (j_row == cur) | (j_row == cur - 1)
    score = jnp.where(forced, FORCE_SCORE, jnp.where(j_row <= cur, imp_t, -1.0))
    ngrp = nsb // 8
    grp = [score[8 * a:8 * a + 8] for a in range(ngrp)]
    beats = [jnp.zeros((8, tq), F32) for _ in range(ngrp)]
    sub8 = lax.broadcasted_iota(jnp.int32, (8, tq), 0)
    for jp in range(nsb):
        row = score[jp:jp + 1, :]
        for a in range(ngrp):
            if a > jp // 8:
                hit = jnp.where(row >= grp[a], 1.0, 0.0)
            elif a < jp // 8:
                hit = jnp.where(row > grp[a], 1.0, 0.0)
            else:
                hit = jnp.where(sub8 > jp % 8, jnp.where(row >= grp[a], 1.0, 0.0), jnp.where(row > grp[a], 1.0, 0.0))
            beats[a] = beats[a] + hit
    pen_t = [jnp.where(b < topk, 0.0, PENALTY) for b in beats]
    zeros_t = jnp.zeros((NSA_DH, tq), F32)
    pen_t = jnp.concatenate([zeros_t] + pen_t + ([zeros_t[:LANE - NSA_DH - nsb]] if nsb < LANE - NSA_DH else []), axis=0)
    pen = pen_t.T.astype(BF16)
    lane = lax.broadcasted_iota(jnp.int32, (tq, LANE), 1)
    for g in range(G):
        qa_ref[g * tq:(g + 1) * tq, :] = jnp.where(lane < NSA_DH, qr_ref[:, g * LANE:(g + 1) * LANE], pen)

    m_s_ref[...] = jnp.full((R, LANE), NEG, F32)
    acc_s_ref[...] = jnp.zeros((R, V_SLOT), F32)
    qa = qa_ref[...]
    top = pl.multiple_of((i // 2) * tk, tk)
    _attend(qa, ksl_ref[pl.ds(top, tk), :], vsl_ref[pl.ds(top, tk), :], position_bias(top, tk), acc_s_ref, m_s_ref)

    def big_tile(j, carry):
        r = pl.multiple_of(j * tk, tk)
        _attend(qa, ksl_ref[pl.ds(r, tk), :], vsl_ref[pl.ds(r, tk), :], None, acc_s_ref, m_s_ref)
        return carry

    lax.fori_loop(0, i // 2, big_tile, 0)

    sig = _sigmoid(ng_ref[...])
    sig_hi = sig.astype(BF16)
    sig_lo = (sig - sig_hi.astype(F32)).astype(BF16)
    gates = _dot(sig_hi, gexp_ref[...]) + _dot(sig_lo, gexp_ref[...])
    low = lane < NSA_DH
    any_valid = jnp.where(t0 + lax.broadcasted_iota(jnp.int32, (tq, LANE), 0) >= CMP_BLOCK - 1, 1.0, 0.0)
    for pair in range(G // 2):
        ev = slice(2 * pair * tq, (2 * pair + 1) * tq)
        od = slice((2 * pair + 1) * tq, (2 * pair + 2) * tq)

        def gate(branch):
            lo = (branch * (G // 2) + pair) * LANE
            return gates[:, lo:lo + LANE]

        def normalised(acc):
            even, odd = acc[ev, :], acc[od, :]
            even_sw, odd_sw = pltpu.roll(even, NSA_DH, 1), pltpu.roll(odd, NSA_DH, 1)
            return jnp.where(low, even, odd_sw) / jnp.where(low, even_sw, odd)

        o = (gate(0) * any_valid * normalised(o_cmp) + gate(1) * normalised(acc_s_ref)
             + gate(2) * normalised(acc_w_ref))
        y_ref[:, pair * LANE:(pair + 1) * LANE] = o * _silu(nz_ref[:, pair * LANE:(pair + 1) * LANE])


def _nsa_call(qn, qr, kc, vc, ksl, vsl, kwn, vwn, ng, nz, ovt, gexp, tq, tk):
    bsz, seq, _ = qn.shape
    ncp = kc.shape[1]
    rows = NSA_G * tq
    per_kv = lambda nrows, width: pl.BlockSpec((None, nrows, width), lambda b, k, i: (b, 0, k))
    qspec = pl.BlockSpec((None, tq, NSA_G * LANE), lambda b, k, i: (b, i, k))
    yspec = pl.BlockSpec((None, tq, NSA_G * NSA_DH), lambda b, k, i: (b, i, k))
    return pl.pallas_call(
        functools.partial(_nsa_kernel, tq=tq, tk=tk),
        grid=(bsz, NSA_KV_HEADS, seq // tq),
        in_specs=[qspec, qspec, per_kv(ncp, LANE), per_kv(ncp, V_SLOT), per_kv(seq, LANE), per_kv(seq, V_SLOT),
                  per_kv(seq, LANE), per_kv(seq, V_SLOT),
                  pl.BlockSpec((None, tq, LANE), lambda b, k, i: (b, i, 0)), yspec,
                  pl.BlockSpec(ovt.shape, lambda b, k, i: (0, 0)),
                  pl.BlockSpec((None,) + gexp.shape[1:], lambda b, k, i: (k, 0, 0))],
        out_specs=yspec,
        out_shape=jax.ShapeDtypeStruct((bsz, seq, NSA_W), F32),
        scratch_shapes=[pltpu.VMEM((rows, LANE), BF16), pltpu.VMEM((rows, V_SLOT), F32), pltpu.VMEM((rows, LANE), F32),
                        pltpu.VMEM((rows, V_SLOT), F32)],
        compiler_params=pltpu.CompilerParams(dimension_semantics=("parallel", "parallel", "arbitrary"),
                                             vmem_limit_bytes=VMEM_LIMIT),
        name="nsa_attn",
    )(qn, qr, kc, vc, ksl, vsl, kwn, vwn, ng, nz, ovt, gexp)


def _out_kernel(h_ref, yh_ref, yn_ref, p_ref, wo_ref, g_ref, wpg_ref, wpp_ref, o_ref):
    h = h_ref[...] + _dot(yh_ref[...].astype(BF16), wo_ref[0:HG_W, :]) + _dot(yn_ref[...].astype(BF16), wo_ref[HG_W:, :])
    var = jnp.mean(h * h, axis=-1, keepdims=True)
    hn = (h * lax.rsqrt(var + RMS_EPS) * g_ref[...]).astype(BF16)
    gate = _sigmoid(_dot(hn, wpg_ref[...]))
    o_ref[...] = h + gate * _dot(p_ref[...].astype(BF16), wpp_ref[...])


def _out_call(h2, yh, yn, p2, wo, g, wpg, wpp, tm):
    n = h2.shape[0]
    row = lambda width: pl.BlockSpec((tm, width), lambda i: (i, 0))
    const = lambda a: pl.BlockSpec(a.shape, lambda i: (0, 0))
    return pl.pallas_call(
        _out_kernel,
        grid=(n // tm,),
        in_specs=[row(D_MODEL), row(HG_W), row(NSA_W), row(PLE_DIM), const(wo), const(g), const(wpg), const(wpp)],
        out_specs=row(D_MODEL),
        out_shape=jax.ShapeDtypeStruct((n, D_MODEL), F32),
        compiler_params=pltpu.CompilerParams(dimension_semantics=("parallel",), vmem_limit_bytes=VMEM_LIMIT),
        name="out_proj_ple",
    )(h2, yh, yn, p2, wo, g, wpg, wpp)


def _pad_last(a, width):
    return jnp.pad(a, [(0, 0)] * (a.ndim - 1) + [(0, width - a.shape[-1])])


def _layout_w_in(w_in):
    n_gate = 3 * NSA_HEADS
    body, ngate, nz = w_in[..., :C_NG], w_in[..., C_NG:C_NG + n_gate], w_in[..., C_NG + n_gate:]
    return jnp.concatenate([body, _pad_last(ngate, LANE), nz], axis=-1).astype(BF16)


def _layout_cmp(pe, w1, w2):
    kvh, half = NSA_KV_HEADS, CMP_BLOCK // 2
    eye = jnp.eye(kvh, dtype=F32)
    w1r = w1.reshape(2, half, NSA_DH, CMP_HIDDEN)
    w1x = jnp.einsum('ardh,gk->argdkh', w1r, eye)
    w1x = w1x.reshape(2, half * kvh * NSA_DH, kvh * CMP_HIDDEN).astype(BF16)
    w2s = _pad_last(w2, LANE)
    w2x = jnp.einsum('hs,gk->ghks', w2s, eye).reshape(kvh * CMP_HIDDEN, kvh * w2s.shape[1]).astype(BF16)
    pe2 = jnp.broadcast_to(pe.reshape(2, half, 1, NSA_DH), (2, half, kvh, NSA_DH)).reshape(2, half * kvh * NSA_DH)
    return pe2, w1x[0], w1x[1], w2x


def kernel(x, p, norm_g, w_in, hgrn_lb, hgrn_onorm_g, nsa_qnorm_g, nsa_knorm_g, cmp_pe, cmp_w1, cmp_w2, w_out,
           ple_norm_g, w_pg, w_pp):
    bsz, seq, _ = x.shape
    depth = w_in.shape[0]
    n = bsz * seq
    tm = 256
    tq, tk, tb = WINDOW // 2, WINDOW, 512
    assert seq % tk == 0 and seq // SLC_BLOCK <= LANE - NSA_DH and (seq // SLC_BLOCK) % 8 == 0
    ncp = seq // CMP_STRIDE
    nsb = seq // SLC_BLOCK

    pos = jnp.arange(seq, dtype=F32)
    inv = ROPE_THETA ** (-jnp.arange(0, ROPE_DIM, 2, dtype=F32) / ROPE_DIM)
    ang = pos[:, None] * inv[None, :]
    cos, sin = jnp.cos(ang), jnp.sin(ang)
    half = ROPE_DIM // 2
    per_head = lambda t: jnp.tile(_pad_last(t, NSA_DH), (1, LANE // NSA_DH))
    cos_t = per_head(jnp.concatenate([cos, cos, jnp.ones((seq, NSA_DH - ROPE_DIM), F32)], axis=1))
    sa_t = per_head(-sin)
    sb_t = per_head(jnp.concatenate([jnp.zeros((seq, half), F32), sin], axis=1))

    lb_all = jnp.cumsum(jax.nn.softmax(hgrn_lb.astype(F32), axis=0), axis=0)
    lb_all = lb_all - lb_all[0]
    lbv = jnp.stack([jnp.log(lb_all), jnp.log1p(-lb_all)], axis=1)

    c_tok = jnp.arange(ncp)[:, None] * CMP_STRIDE + jnp.arange(CMP_BLOCK)[None, :]
    overlap = jnp.mean((c_tok[..., None] // SLC_BLOCK == jnp.arange(nsb)).astype(F32), axis=1)
    ovt = jnp.concatenate([overlap.T, jnp.ones((8, ncp), F32)], axis=0).astype(BF16)
    col = jnp.arange(3 * (NSA_G // 2) * LANE)
    src = (col // (NSA_G // 2 * LANE)) * NSA_HEADS + ((col // LANE) % (NSA_G // 2)) * 2 + (col % LANE) // NSA_DH
    gexp = (jnp.arange(LANE)[None, :, None] == src[None, None, :] + NSA_G * jnp.arange(NSA_KV_HEADS)[:, None, None]).astype(BF16)

    w_all = _layout_w_in(w_in)
    wo16, wpg16, wpp16 = w_out.astype(BF16), w_pg.astype(BF16), w_pp.astype(BF16)
    qg_all = jnp.tile(nsa_qnorm_g, (1, LANE // NSA_DH))
    kg_tiled = jnp.tile(nsa_knorm_g, (1, 1, LANE // NSA_DH))
    kg_all = _pad_last(nsa_knorm_g, LANE)

    h = x.reshape(n, D_MODEL)
    for i in range(depth):
        r3 = lambda a: a.reshape(bsz, seq, -1)
        (hq, lf, lk, hv, hgate, qn, qr, kcm, vcm, ksl, vsl, kwn, vwn, ng, nz) = _proj_call(
            h, norm_g[i][None], w_all[i], cos_t, sa_t, sb_t, qg_all[i][None], kg_tiled[i, 1:3], lbv[i], seq, tm)
        y_hg = _hgrn_call(r3(hq), r3(lf), r3(lk), r3(hv), r3(hgate), hgrn_onorm_g[i][None], tb)
        seg_w = CMP_STRIDE * KV_W
        kc = _cmp_call(kcm.reshape(bsz, ncp, seg_w), *_layout_cmp(cmp_pe[i, 0], cmp_w1[i, 0], cmp_w2[i, 0]),
                       kg_all[i, 0:1], True)
        vc = _cmp_call(vcm.reshape(bsz, ncp, seg_w), *_layout_cmp(cmp_pe[i, 1], cmp_w1[i, 1], cmp_w2[i, 1]),
                       kg_all[i, 0:1], False)
        y_nsa = _nsa_call(r3(qn), r3(qr), kc, vc, r3(ksl), r3(vsl), r3(kwn), r3(vwn), r3(ng), r3(nz), ovt, gexp,
                          tq, tk)
        h = _out_call(h, y_hg.reshape(n, HG_W), y_nsa.reshape(n, NSA_W), p[i].reshape(n, PLE_DIM), wo16[i],
                      ple_norm_g[i][None], wpg16[i], wpp16[i], tm)
    return h.reshape(bsz, seq, D_MODEL)
```

```python
import functools

import jax
import jax.numpy as jnp
from jax import lax
from jax.experimental import pallas as pl
from jax.experimental.pallas import tpu as pltpu

F32 = jnp.float32
BF16 = jnp.bfloat16

D_MODEL = 1024
HG_HEADS = 4
HG_DK = 128
HG_DV = 128
HG_W = HG_HEADS * HG_DV
HG_CHUNK = 64
HG_SUB = 16
NSA_HEADS = 8
NSA_KV_HEADS = 2
NSA_DH = 64
NSA_G = NSA_HEADS // NSA_KV_HEADS
NSA_W = NSA_HEADS * NSA_DH
KV_W = NSA_KV_HEADS * NSA_DH
CMP_BLOCK = 32
CMP_STRIDE = 16
CMP_HIDDEN = 128
SLC_BLOCK = 64
SLC_TOPK = 16
WINDOW = 512
FORCE_SCORE = 1e4
ROPE_THETA = 500000.0
ROPE_DIM = NSA_DH // 4
PLE_DIM = 256
RMS_EPS = 1e-6
IN_SPLITS = (HG_W,) * 5 + (NSA_W,) + (KV_W,) * 6 + (3 * NSA_HEADS, NSA_W)

LANE = 128
SUBLANES = 8
NEG = -1e30
LOG2E = 1.4426950408889634
SLC_SHIFT = SLC_BLOCK.bit_length() - 1
MXU_COLS = 256
PENALTY = -(2.0 ** 100)
VMEM_LIMIT = 56 * 1024 * 1024

V_SLOT = LANE

C_HG = 0
C_Q = C_HG + 5 * HG_W
C_KCM = C_Q + NSA_W
C_KSL = C_KCM + 2 * KV_W
C_KWN = C_KSL + 2 * KV_W
C_NG = C_KWN + 2 * KV_W
C_NZ = C_NG + LANE
C_END = C_NZ + NSA_W


def _dot(a, b):
    return jnp.dot(a, b, preferred_element_type=F32)


def _dot_nt(a, b):
    return lax.dot_general(a, b, (((1,), (1,)), ((), ())), preferred_element_type=F32)


def _dot_tn(a, b):
    return lax.dot_general(a, b, (((0,), (0,)), ((), ())), preferred_element_type=F32)


def _sigmoid(x):
    return 1.0 / (1.0 + jnp.exp(-x))


def _silu(x):
    return x * _sigmoid(x)


def _slot_norm(x, g):
    ms = jnp.sum(x * x, axis=-1, keepdims=True) * (1.0 / NSA_DH)
    return x * lax.rsqrt(ms + RMS_EPS) * g


def _proj_kernel(h_ref, g_ref, w_ref, cos_ref, sa_ref, sb_ref, qg_ref, kg_ref, lb_ref,
                 hq_ref, lf_ref, lk_ref, hv_ref, hgate_ref, qn_ref, qr_ref, kcm_ref, vcm_ref, ksl_ref, vsl_ref,
                 kwn_ref, vwn_ref, ng_ref, nz_ref, *, tiles_per_seq):
    x = h_ref[...]
    var = jnp.mean(x * x, axis=-1, keepdims=True)
    xn = (x * lax.rsqrt(var + RMS_EPS) * g_ref[...]).astype(BF16)

    def mm(a, b):
        return _dot(xn, w_ref[:, a:b])

    W = HG_W
    tm = x.shape[0]

    def decay_prep(fl, head):
        cols = slice(head * LANE, (head + 1) * LANE)
        loglb, l1mlb = lb_ref[0:1, cols], lb_ref[1:2, cols]
        l1pe = jnp.log(1.0 + jnp.exp(-jnp.abs(fl)))
        cc = l1mlb + (jnp.minimum(fl, 0.0) - l1pe)
        lf = (jnp.maximum(loglb, cc) + jnp.log(1.0 + jnp.exp(-jnp.abs(loglb - cc)))) * LOG2E
        lk = (l1mlb - jnp.maximum(fl, 0.0) - l1pe) * LOG2E
        t3 = lf.reshape(tm // SUBLANES, SUBLANES, LANE)
        sub = lax.broadcasted_iota(jnp.int32, t3.shape, 1)
        for step in (1, 2, 4):
            t3 = t3 + jnp.where(sub >= step, pltpu.roll(t3, step, 1), 0.0)
        per_chunk = HG_CHUNK // SUBLANES
        for tile in range(tm // SUBLANES):
            rows = slice(tile * SUBLANES, (tile + 1) * SUBLANES)
            total = t3[tile, SUBLANES - 1:SUBLANES, :]
            if tile % per_chunk == 0:
                b, run = t3[tile], total
            else:
                b, run = t3[tile] + run, run + total
            lf_ref[rows, cols] = b
            lk_ref[rows, cols] = b - lk[rows]

    fl_a = mm(C_HG + W, C_HG + W + W // 2)
    qs = mm(C_Q, C_KCM)
    decay_prep(fl_a[:, :LANE], 0)
    kvc = mm(C_KCM, C_KSL)
    kv_s = mm(C_KSL, C_KWN)
    decay_prep(fl_a[:, LANE:], 1)
    fl_b = mm(C_HG + W + W // 2, C_HG + 2 * W)
    kv_w = mm(C_KWN, C_NG)
    gz = mm(C_NG, C_END)
    decay_prep(fl_b[:, :LANE], 2)
    hq_ref[...] = mm(C_HG, C_HG + W)
    decay_prep(fl_b[:, LANE:], 3)
    hv_ref[...] = mm(C_HG + 2 * W, C_HG + 3 * W).astype(BF16)

    c, sa, sb = cos_ref[...], sa_ref[...], sb_ref[...]
    scale = NSA_DH ** -0.5 * LOG2E
    lane = lax.broadcasted_iota(jnp.int32, (tm, LANE), 1)
    low = lane < NSA_DH
    kcm_ref[...] = kvc[:, :KV_W]
    vcm_ref[...] = kvc[:, KV_W:]
    ng_ref[...] = gz[:, :LANE]
    nz_ref[...] = gz[:, LANE:]

    nq = NSA_HEADS // 2
    tiles = [qs[:, pr * LANE:(pr + 1) * LANE] for pr in range(nq)] + [kv_s[:, :KV_W], kv_w[:, :KV_W]]
    gains = [qg_ref[...] * scale] * nq + [kg_ref[0:1, :], kg_ref[1:2, :]]
    half = len(tiles) // 2
    sq = jnp.concatenate([jnp.concatenate([t * t for t in tiles[:half]], axis=0),
                          jnp.concatenate([t * t for t in tiles[half:]], axis=0)], axis=1)
    same_head = (jnp.right_shift(lax.broadcasted_iota(jnp.int32, (2 * LANE, 2 * LANE), 0), NSA_DH.bit_length() - 1)
                 == jnp.right_shift(lax.broadcasted_iota(jnp.int32, (2 * LANE, 2 * LANE), 1), NSA_DH.bit_length() - 1))
    head_sum = jnp.where(same_head, 1.0, 0.0).astype(BF16)
    sq_hi = sq.astype(BF16)
    sq_lo = (sq - sq_hi.astype(F32)).astype(BF16)
    ssq = _dot(sq_hi, head_sum) + _dot(sq_lo, head_sum)
    ms = [ssq[(j % half) * tm:(j % half + 1) * tm, (j // half) * LANE:(j // half + 1) * LANE] * (1.0 / NSA_DH)
          for j in range(len(tiles))]
    normed = [t * lax.rsqrt(m + RMS_EPS) * g for t, m, g in zip(tiles, ms, gains)]
    roped = [t * c + pltpu.roll(t, LANE - ROPE_DIM // 2, 1) * sa + pltpu.roll(t, ROPE_DIM // 2, 1) * sb for t in normed]

    def head_slots(tile):
        return jnp.where(low, tile, 0.0), jnp.where(low, pltpu.roll(tile, NSA_DH, 1), 0.0)

    pos = (pl.program_id(0) % tiles_per_seq) * tm + lax.broadcasted_iota(jnp.int32, (tm, LANE), 0)
    blk_onehot = jnp.where(lane - NSA_DH == jnp.right_shift(pos, SLC_SHIFT), 1.0, 0.0)
    for pr in range(nq):
        for k, (a, b) in enumerate(zip(head_slots(normed[pr]), head_slots(roped[pr]))):
            sl = slice((2 * pr + k) * LANE, (2 * pr + k + 1) * LANE)
            qn_ref[:, sl] = a.astype(BF16)
            qr_ref[:, sl] = b.astype(BF16)
    for kv, (ks, kw) in enumerate(zip(head_slots(roped[nq]), head_slots(roped[nq + 1]))):
        sl = slice(kv * LANE, (kv + 1) * LANE)
        ksl_ref[:, sl] = jnp.where(low, ks, blk_onehot).astype(BF16)
        kwn_ref[:, sl] = kw.astype(BF16)
    for ref, t in ((vsl_ref, kv_s[:, KV_W:]), (vwn_ref, kv_w[:, KV_W:])):
        for kv, v in enumerate((t, pltpu.roll(t, NSA_DH, 1))):
            ref[:, kv * V_SLOT:(kv + 1) * V_SLOT] = jnp.where(low, v, 1.0).astype(BF16)

    hgate_ref[...] = _sigmoid(mm(C_HG + 3 * W, C_HG + 4 * W)) * _silu(mm(C_HG + 4 * W, C_HG + 5 * W))


def _of_layer(a, layer, **kw):
    return pl.BlockSpec((None,) + a.shape[1:], lambda *_: (layer,) + (0,) * (a.ndim - 1), **kw)


def _proj_call(h2, g, w, cos_t, sa_t, sb_t, qg, kg, lbv, layer, seq, tm):
    n = h2.shape[0]
    tpb = seq // tm
    row = lambda width: pl.BlockSpec((tm, width), lambda i: (i, 0))
    tab = pl.BlockSpec((tm, LANE), lambda i: (i % tpb, 0))
    k_slots, v_slots = NSA_KV_HEADS * LANE, NSA_KV_HEADS * V_SLOT
    out_w = [(HG_W, F32), (HG_W, F32), (HG_W, F32), (HG_W, BF16), (HG_W, F32),
             (NSA_HEADS * LANE, BF16), (NSA_HEADS * LANE, BF16), (KV_W, F32), (KV_W, F32),
             (k_slots, BF16), (v_slots, BF16), (k_slots, BF16), (v_slots, BF16), (LANE, F32), (NSA_W, F32)]
    return pl.pallas_call(
        functools.partial(_proj_kernel, tiles_per_seq=tpb),
        grid=(n // tm,),
        in_specs=[row(D_MODEL), _of_layer(g, layer), _of_layer(w, layer, pipeline_mode=pl.Buffered(1)), tab, tab, tab,
                  _of_layer(qg, layer), _of_layer(kg, layer), _of_layer(lbv, layer)],
        out_specs=[row(wd) for wd, _ in out_w],
        out_shape=[jax.ShapeDtypeStruct((n, wd), dt) for wd, dt in out_w],
        compiler_params=pltpu.CompilerParams(dimension_semantics=("parallel",), vmem_limit_bytes=VMEM_LIMIT),
        name="in_proj",
    )(h2, g, w, cos_t, sa_t, sb_t, qg, kg, lbv)


def _hgrn_kernel(q_ref, b_ref, c_ref, v_ref, gate_ref, og_ref, y_ref, st_ref, u_ref, *, tb):
    C, SUB, HALF = HG_CHUNK, HG_SUB, HG_SUB // 2
    nch, nblk = tb // C, C // SUB

    @pl.when(pl.program_id(2) == 0)
    def _():
        st_ref[...] = jnp.zeros_like(st_ref)

    og = og_ref[...]
    lane8 = lax.broadcasted_iota(jnp.int32, (HALF, LANE), 1)
    lane_c = lax.broadcasted_iota(jnp.int32, (SUB, C), 1)
    row_c = lax.broadcasted_iota(jnp.int32, (SUB, C), 0)
    left_col = jnp.where(lax.broadcasted_iota(jnp.int32, (2 * HG_DK, 2 * LANE), 1) < LANE, 1.0, 0.0)
    sum_w = jnp.where(lax.broadcasted_iota(jnp.int32, (2 * HG_DK, 2 * LANE), 0) < HG_DK,
                      left_col, 1.0 - left_col).astype(BF16)

    q = [q_ref[k * C:(k + 1) * C, :] for k in range(nch)]
    b = [b_ref[k * C:(k + 1) * C, :] for k in range(nch)]
    c = [c_ref[k * C:(k + 1) * C, :] for k in range(nch)]
    v16 = [v_ref[k * C:(k + 1) * C, :] for k in range(nch)]

    left, right = [], []
    for k in range(nch):
        for blk in range(nblk):
            r0 = blk * SUB
            q_t, q_b = q[k][r0:r0 + HALF], q[k][r0 + HALF:r0 + SUB]
            b_t, b_b = b[k][r0:r0 + HALF], b[k][r0 + HALF:r0 + SUB]
            tops = [q_t * jnp.exp2(b_t - c[k][r0 + s:r0 + s + 1, :]) for s in range(HALF)]
            bots = [q_b * jnp.exp2(b_b - c[k][r0 + s:r0 + s + 1, :]) for s in range(SUB)]
            left += tops + bots[:HALF // 2]
            right += bots[HALF // 2:]
    per_blk = HALF + HALF // 2
    xs = jnp.concatenate([jnp.concatenate(left, axis=0), jnp.concatenate(right, axis=0)], axis=1)
    rs = _dot(xs.astype(BF16), sum_w)

    a = []
    for k in range(nch):
        a_rows = []
        for blk in range(nblk):
            r0 = blk * SUB
            base = (k * nblk + blk) * per_blk * HALF

            def piece(idx, side):
                return rs[base + idx * HALF:base + (idx + 1) * HALF, side * LANE:(side + 1) * LANE]

            a_t = jnp.zeros((HALF, LANE), F32)
            a_b = jnp.zeros((HALF, LANE), F32)
            for s in range(HALF):
                a_t = jnp.where(lane8 == s, piece(s, 0), a_t)
            for s in range(SUB):
                src = piece(HALF + s, 0) if s < HALF // 2 else piece(s - HALF // 2, 1)
                a_b = jnp.where(lane8 == s, src, a_b)
            a_d = jnp.concatenate([a_t, a_b], axis=0)
            if blk > 0:
                ref_b = b[k][r0 - 1:r0, :]
                q_p = (q[k][r0:r0 + SUB] * jnp.exp2(b[k][r0:r0 + SUB] - ref_b)).astype(BF16)
                k_p = jnp.concatenate([jnp.exp2(ref_b - c[k][:r0]), jnp.zeros((C - r0, HG_DK), F32)], axis=0)
                a_i = pltpu.roll(a_d, r0, 1)[:, :C] + _dot_nt(q_p, k_p.astype(BF16))
            else:
                a_i = a_d[:, :C]
            a_rows.append(jnp.where(lane_c <= r0 + row_c, a_i, 0.0))
        a.append(jnp.concatenate(a_rows, axis=0).astype(BF16))

    for k in range(nch):
        u_ref[k] = _dot_tn(v16[k], jnp.exp2(b[k][C - 1:C, :] - c[k]).astype(BF16))

    st = st_ref[...]
    for k in range(nch):
        o = _dot(a[k], v16[k]) + _dot_nt((q[k] * jnp.exp2(b[k])).astype(BF16), st.astype(BF16))
        st = st * jnp.exp2(b[k][C - 1:C, :]) + u_ref[k]
        ms = jnp.mean(o * o, axis=-1, keepdims=True)
        y_ref[k * C:(k + 1) * C, :] = o * lax.rsqrt(ms + RMS_EPS) * og * gate_ref[k * C:(k + 1) * C, :]
    st_ref[...] = st


def _hgrn_call(hq, lf, lk, hv, hgate, og, layer, tb):
    bsz, seq, _ = hq.shape
    piece = pl.BlockSpec((None, tb, HG_DK), lambda b, h, i: (b, i, h))
    return pl.pallas_call(
        functools.partial(_hgrn_kernel, tb=tb),
        grid=(bsz, HG_HEADS, seq // tb),
        in_specs=[piece, piece, piece, piece, piece, _of_layer(og, layer)],
        out_specs=piece,
        out_shape=jax.ShapeDtypeStruct((bsz, seq, HG_W), F32),
        scratch_shapes=[pltpu.VMEM((HG_DV, HG_DK), F32), pltpu.VMEM((tb // HG_CHUNK, HG_DV, HG_DK), F32)],
        compiler_params=pltpu.CompilerParams(dimension_semantics=("parallel", "parallel", "arbitrary"),
                                             vmem_limit_bytes=VMEM_LIMIT),
        name="hgrn2",
    )(hq, lf, lk, hv, hgate, og)


def _cmp_kernel(kcm_ref, vcm_ref, pe_ref, w1_ref, w2_ref, g_ref, kc_ref, vc_ref):
    half = CMP_BLOCK // 2
    ncp = kcm_ref.shape[0] // CMP_STRIDE
    for which, (src, out_ref) in enumerate(((kcm_ref, kc_ref), (vcm_ref, vc_ref))):
        first = jnp.zeros((ncp, NSA_KV_HEADS * CMP_HIDDEN), F32)
        second = jnp.zeros((ncp, NSA_KV_HEADS * CMP_HIDDEN), F32)
        for r in range(half):
            tok = src[pl.ds(r, ncp, stride=CMP_STRIDE), :]
            first = first + _dot((tok + pe_ref[which, r:r + 1, :]).astype(BF16), w1_ref[which, r])
            second = second + _dot((tok + pe_ref[which, half + r:half + r + 1, :]).astype(BF16), w1_ref[which, half + r])
        hid = first + pltpu.roll(second, ncp - 1, 0)
        out = _dot(_silu(hid).astype(BF16), w2_ref[which])
        if which == 0:
            for kv in range(NSA_KV_HEADS):
                sl = slice(kv * LANE, (kv + 1) * LANE)
                out_ref[:, sl] = _slot_norm(out[:, sl], g_ref[...]).astype(BF16)
        else:
            head_lane = jnp.bitwise_and(lax.broadcasted_iota(jnp.int32, out.shape, 1), LANE - 1)
            out_ref[...] = jnp.where(head_lane < NSA_DH, out, 1.0).astype(BF16)


def _cmp_call(kcm, vcm, pe, w1, w2, g, layer):
    bsz, seq, _ = kcm.shape
    ncp = seq // CMP_STRIDE
    slots = NSA_KV_HEADS * LANE
    tok = pl.BlockSpec((None, seq, KV_W), lambda b: (b, 0, 0))
    out = pl.BlockSpec((None, ncp, slots), lambda b: (b, 0, 0))
    return pl.pallas_call(
        _cmp_kernel,
        grid=(bsz,),
        in_specs=[tok, tok, _of_layer(pe, layer), _of_layer(w1, layer), _of_layer(w2, layer), _of_layer(g, layer)],
        out_specs=[out, out],
        out_shape=[jax.ShapeDtypeStruct((bsz, ncp, slots), BF16)] * 2,
        compiler_params=pltpu.CompilerParams(dimension_semantics=("parallel",), vmem_limit_bytes=VMEM_LIMIT),
        name="compress",
    )(kcm, vcm, pe, w1, w2, g)


def _attend(q, k, v, bias, acc_ref, m_ref):
    p, m_new = _softmax_tiles(q, k, bias, m_ref[...])
    acc_ref[...] = jnp.exp2(m_ref[...] - m_new) * acc_ref[...] + _pv_tiles(p, v)
    m_ref[...] = m_new


def _softmax_tiles(q, k, bias, m_floor):
    pieces = []
    for c in range(k.shape[0] // MXU_COLS):
        s = _dot_nt(q, k[c * MXU_COLS:(c + 1) * MXU_COLS])
        pieces.append(s if bias is None else s + bias[:, c * MXU_COLS:(c + 1) * MXU_COLS])
    row_max = functools.reduce(jnp.maximum, [jnp.max(s, axis=-1, keepdims=True) for s in pieces])
    m = row_max if m_floor is None else jnp.maximum(m_floor, row_max)
    m_wide = jnp.concatenate([jnp.broadcast_to(m, (q.shape[0], LANE))] * (MXU_COLS // LANE), axis=1)
    return [jnp.exp2(s - m_wide).astype(BF16) for s in pieces], m


def _pv_tiles(p, v):
    return functools.reduce(lambda a, b: a + b,
                            [_dot(pc, v[c * MXU_COLS:(c + 1) * MXU_COLS]) for c, pc in enumerate(p)])


def _nsa_kernel(qn_ref, qr_ref, kc_ref, vc_ref, ksl_ref, vsl_ref, kwn_ref, vwn_ref, ng_ref, nz_ref,
                ovt_ref, gexp_ref, y_ref, qa_ref, acc_s_ref, m_s_ref, acc_w_ref, *, tq, tk):
    G = NSA_G
    R = G * tq
    i = pl.program_id(2)
    t0 = i * tq
    ncp = kc_ref.shape[0]
    nsb = ovt_ref.shape[0] - 8
    topk = min(SLC_TOPK, nsb)

    def heads_on_rows(ref):
        return jnp.concatenate([ref[:, g * LANE:(g + 1) * LANE] for g in range(G)], axis=0)

    def all_heads(a):
        return jnp.concatenate([a] * G, axis=0)

    q_pos = t0 + lax.broadcasted_iota(jnp.int32, (tq, 1), 0)

    def position_bias(start, width, window=None):
        dist = q_pos - (start + lax.broadcasted_iota(jnp.int32, (tq, width), 1))
        ok = jnp.where(dist >= 0, 0.0, NEG)
        return all_heads(ok if window is None else jnp.where(dist < window, ok, NEG))

    q4 = heads_on_rows(qr_ref)
    w0 = pl.multiple_of(jnp.maximum(t0 - 2 * tq, 0), tq)
    p_w, _ = _softmax_tiles(q4, kwn_ref[pl.ds(w0, 3 * tq), :], position_bias(w0, 3 * tq, WINDOW), None)
    acc_w_ref[...] = _pv_tiles(p_w, vwn_ref[pl.ds(w0, 3 * tq), :])

    c_end = lax.broadcasted_iota(jnp.int32, (tq, ncp), 1) * CMP_STRIDE + (CMP_BLOCK - 1)
    s = _dot_nt(heads_on_rows(qn_ref), kc_ref[...]) + all_heads(jnp.where(c_end <= q_pos, 0.0, NEG))
    e = jnp.exp2(s - jnp.max(s, axis=-1, keepdims=True))
    e_hi = e.astype(BF16)
    e_lo = (e - e_hi.astype(F32)).astype(BF16)
    o_cmp = _dot(e_hi, vc_ref[...])

    ovt = ovt_ref[...]
    raw_t = _dot_nt(ovt, e_hi) + _dot_nt(ovt, e_lo)
    any_valid_t = jnp.where(t0 + lax.broadcasted_iota(jnp.int32, (1, tq), 1) >= CMP_BLOCK - 1, 1.0, 0.0)
    imp_t = jnp.zeros((nsb, tq), F32)
    for g in range(G):
        imp_t = imp_t + raw_t[:nsb, g * tq:(g + 1) * tq] * (any_valid_t / raw_t[nsb:nsb + 1, g * tq:(g + 1) * tq])
    j_row = lax.broadcasted_iota(jnp.int32, (nsb, tq), 0)
    cur = jnp.right_shift(t0 + lax.broadcasted_iota(jnp.int32, (nsb, tq), 1), SLC_SHIFT)
    forced = (j_row == 0) | (j_row == cur) | (j_row == cur - 1)
    score = jnp.where(forced, FORCE_SCORE, jnp.where(j_row <= cur, imp_t, -1.0))
    ngrp = nsb // 8
    grp = [score[8 * a:8 * a + 8] for a in range(ngrp)]
    beats = [jnp.zeros((8, tq), F32) for _ in range(ngrp)]
    sub8 = lax.broadcasted_iota(jnp.int32, (8, tq), 0)
    for jp in range(nsb):
        row = score[jp:jp + 1, :]
        for a in range(ngrp):
            if a > jp // 8:
                hit = jnp.where(row >= grp[a], 1.0, 0.0)
            elif a < jp // 8:
                hit = jnp.where(row > grp[a], 1.0, 0.0)
            else:
                hit = jnp.where(sub8 > jp % 8, jnp.where(row >= grp[a], 1.0, 0.0), jnp.where(row > grp[a], 1.0, 0.0))
            beats[a] = beats[a] + hit
    pen_t = [jnp.where(b < topk, 0.0, PENALTY) for b in beats]
    zeros_t = jnp.zeros((NSA_DH, tq), F32)
    pen_t = jnp.concatenate([zeros_t] + pen_t + ([zeros_t[:LANE - NSA_DH - nsb]] if nsb < LANE - NSA_DH else []), axis=0)
    pen = pen_t.T.astype(BF16)
    lane = lax.broadcasted_iota(jnp.int32, (tq, LANE), 1)
    for g in range(G):
        qa_ref[g * tq:(g + 1) * tq, :] = jnp.where(lane < NSA_DH, qr_ref[:, g * LANE:(g + 1) * LANE], pen)

    m_s_ref[...] = jnp.full((R, LANE), NEG, F32)
    acc_s_ref[...] = jnp.zeros((R, V_SLOT), F32)
    qa = qa_ref[...]
    top = pl.multiple_of((i // 2) * tk, tk)
    _attend(qa, ksl_ref[pl.ds(top, tk), :], vsl_ref[pl.ds(top, tk), :], position_bias(top, tk), acc_s_ref, m_s_ref)

    def big_tile(j, carry):
        r = pl.multiple_of(j * tk, tk)
        _attend(qa, ksl_ref[pl.ds(r, tk), :], vsl_ref[pl.ds(r, tk), :], None, acc_s_ref, m_s_ref)
        return carry

    lax.fori_loop(0, i // 2, big_tile, 0)

    sig = _sigmoid(ng_ref[...])
    sig_hi = sig.astype(BF16)
    sig_lo = (sig - sig_hi.astype(F32)).astype(BF16)
    gates = _dot(sig_hi, gexp_ref[...]) + _dot(sig_lo, gexp_ref[...])
    low = lane < NSA_DH
    any_valid = jnp.where(t0 + lax.broadcasted_iota(jnp.int32, (tq, LANE), 0) >= CMP_BLOCK - 1, 1.0, 0.0)
    for pair in range(G // 2):
        ev = slice(2 * pair * tq, (2 * pair + 1) * tq)
        od = slice((2 * pair + 1) * tq, (2 * pair + 2) * tq)

        def gate(branch):
            lo = (branch * (G // 2) + pair) * LANE
            return gates[:, lo:lo + LANE]

        def normalised(acc):
            even, odd = acc[ev, :], acc[od, :]
            even_sw, odd_sw = pltpu.roll(even, NSA_DH, 1), pltpu.roll(odd, NSA_DH, 1)
            return jnp.where(low, even, odd_sw) / jnp.where(low, even_sw, odd)

        o = (gate(0) * any_valid * normalised(o_cmp) + gate(1) * normalised(acc_s_ref)
             + gate(2) * normalised(acc_w_ref))
        y_ref[:, pair * LANE:(pair + 1) * LANE] = o * _silu(nz_ref[:, pair * LANE:(pair + 1) * LANE])


def _nsa_call(qn, qr, kc, vc, ksl, vsl, kwn, vwn, ng, nz, ovt, gexp, tq, tk):
    bsz, seq, _ = qn.shape
    ncp = kc.shape[1]
    rows = NSA_G * tq
    per_kv = lambda nrows, width: pl.BlockSpec((None, nrows, width), lambda b, k, i: (b, 0, k))
    qspec = pl.BlockSpec((None, tq, NSA_G * LANE), lambda b, k, i: (b, i, k))
    yspec = pl.BlockSpec((None, tq, NSA_G * NSA_DH), lambda b, k, i: (b, i, k))
    return pl.pallas_call(
        functools.partial(_nsa_kernel, tq=tq, tk=tk),
        grid=(bsz, NSA_KV_HEADS, seq // tq),
        in_specs=[qspec, qspec, per_kv(ncp, LANE), per_kv(ncp, V_SLOT), per_kv(seq, LANE), per_kv(seq, V_SLOT),
                  per_kv(seq, LANE), per_kv(seq, V_SLOT),
                  pl.BlockSpec((None, tq, LANE), lambda b, k, i: (b, i, 0)), yspec,
                  pl.BlockSpec(ovt.shape, lambda b, k, i: (0, 0)),
                  pl.BlockSpec((None,) + gexp.shape[1:], lambda b, k, i: (k, 0, 0))],
        out_specs=yspec,
        out_shape=jax.ShapeDtypeStruct((bsz, seq, NSA_W), F32),
        scratch_shapes=[pltpu.VMEM((rows, LANE), BF16), pltpu.VMEM((rows, V_SLOT), F32), pltpu.VMEM((rows, LANE), F32),
                        pltpu.VMEM((rows, V_SLOT), F32)],
        compiler_params=pltpu.CompilerParams(dimension_semantics=("parallel", "parallel", "arbitrary"),
                                             vmem_limit_bytes=VMEM_LIMIT),
        name="nsa_attn",
    )(qn, qr, kc, vc, ksl, vsl, kwn, vwn, ng, nz, ovt, gexp)


def _out_kernel(h_ref, yh_ref, yn_ref, p_ref, wo_ref, g_ref, wpg_ref, wpp_ref, o_ref):
    h = h_ref[...] + _dot(yh_ref[...].astype(BF16), wo_ref[0:HG_W, :]) + _dot(yn_ref[...].astype(BF16), wo_ref[HG_W:, :])
    var = jnp.mean(h * h, axis=-1, keepdims=True)
    hn = (h * lax.rsqrt(var + RMS_EPS) * g_ref[...]).astype(BF16)
    gate = _sigmoid(_dot(hn, wpg_ref[...]))
    o_ref[...] = h + gate * _dot(p_ref[...].astype(BF16), wpp_ref[...])


def _out_call(h2, yh, yn, p3, wo, g, wpg, wpp, layer, tm):
    n = h2.shape[0]
    row = lambda width: pl.BlockSpec((tm, width), lambda i: (i, 0))
    return pl.pallas_call(
        _out_kernel,
        grid=(n // tm,),
        in_specs=[row(D_MODEL), row(HG_W), row(NSA_W), pl.BlockSpec((None, tm, PLE_DIM), lambda i: (layer, i, 0)),
                  _of_layer(wo, layer), _of_layer(g, layer), _of_layer(wpg, layer), _of_layer(wpp, layer)],
        out_specs=row(D_MODEL),
        out_shape=jax.ShapeDtypeStruct((n, D_MODEL), F32),
        compiler_params=pltpu.CompilerParams(dimension_semantics=("parallel",), vmem_limit_bytes=VMEM_LIMIT),
        name="out_proj_ple",
    )(h2, yh, yn, p3, wo, g, wpg, wpp)


def _pad_last(a, width):
    return jnp.pad(a, [(0, 0)] * (a.ndim - 1) + [(0, width - a.shape[-1])])


def _layout_w_in(w_in):
    n_gate = 3 * NSA_HEADS
    body, ngate, nz = w_in[..., :C_NG], w_in[..., C_NG:C_NG + n_gate], w_in[..., C_NG + n_gate:]
    return jnp.concatenate([body, _pad_last(ngate, LANE), nz], axis=-1).astype(BF16)


def _layout_cmp(pe, w1, w2):
    kvh = NSA_KV_HEADS
    lead = w1.shape[:2]
    eye = jnp.eye(kvh, dtype=F32)
    w1r = w1.reshape(lead + (CMP_BLOCK, NSA_DH, CMP_HIDDEN))
    w1x = jnp.einsum('nwldh,gk->nwlgdkh', w1r, eye).reshape(lead + (CMP_BLOCK, kvh * NSA_DH, kvh * CMP_HIDDEN))
    w2x = jnp.einsum('nwhs,gk->nwghks', _pad_last(w2, LANE), eye).reshape(lead + (kvh * CMP_HIDDEN, kvh * LANE))
    pe2 = jnp.tile(pe, (1, 1, 1, kvh))
    return pe2, w1x.astype(BF16), w2x.astype(BF16)


def kernel(x, p, norm_g, w_in, hgrn_lb, hgrn_onorm_g, nsa_qnorm_g, nsa_knorm_g, cmp_pe, cmp_w1, cmp_w2, w_out,
           ple_norm_g, w_pg, w_pp):
    bsz, seq, _ = x.shape
    depth = w_in.shape[0]
    n = bsz * seq
    tm = 256
    tq, tk, tb = WINDOW // 2, WINDOW, 1024
    assert seq % tk == 0 and seq // SLC_BLOCK <= LANE - NSA_DH and (seq // SLC_BLOCK) % 8 == 0
    ncp = seq // CMP_STRIDE
    nsb = seq // SLC_BLOCK

    pos = jnp.arange(seq, dtype=F32)
    inv = ROPE_THETA ** (-jnp.arange(0, ROPE_DIM, 2, dtype=F32) / ROPE_DIM)
    ang = pos[:, None] * inv[None, :]
    cos, sin = jnp.cos(ang), jnp.sin(ang)
    half = ROPE_DIM // 2
    per_head = lambda t: jnp.tile(_pad_last(t, NSA_DH), (1, LANE // NSA_DH))
    cos_t = per_head(jnp.concatenate([cos, cos, jnp.ones((seq, NSA_DH - ROPE_DIM), F32)], axis=1))
    sa_t = per_head(-sin)
    sb_t = per_head(jnp.concatenate([jnp.zeros((seq, half), F32), sin], axis=1))

    lb_all = jnp.cumsum(jax.nn.softmax(hgrn_lb.astype(F32), axis=0), axis=0)
    lb_all = lb_all - lb_all[0]
    lbv = jnp.stack([jnp.log(lb_all), jnp.log1p(-lb_all)], axis=1)

    c_tok = jnp.arange(ncp)[:, None] * CMP_STRIDE + jnp.arange(CMP_BLOCK)[None, :]
    overlap = jnp.mean((c_tok[..., None] // SLC_BLOCK == jnp.arange(nsb)).astype(F32), axis=1)
    ovt = jnp.concatenate([overlap.T, jnp.ones((8, ncp), F32)], axis=0).astype(BF16)
    col = jnp.arange(3 * (NSA_G // 2) * LANE)
    src = (col // (NSA_G // 2 * LANE)) * NSA_HEADS + ((col // LANE) % (NSA_G // 2)) * 2 + (col % LANE) // NSA_DH
    gexp = (jnp.arange(LANE)[None, :, None] == src[None, None, :] + NSA_G * jnp.arange(NSA_KV_HEADS)[:, None, None]).astype(BF16)

    w_all = _layout_w_in(w_in)
    wo16, wpg16, wpp16 = w_out.astype(BF16), w_pg.astype(BF16), w_pp.astype(BF16)
    row_vec = lambda a: a[:, None, :]
    qg_all = row_vec(jnp.tile(nsa_qnorm_g, (1, LANE // NSA_DH)))
    kg_tiled = jnp.tile(nsa_knorm_g[:, 1:], (1, 1, LANE // NSA_DH))
    kg_cmp = _pad_last(nsa_knorm_g[:, 0:1], LANE)
    cmp_params = _layout_cmp(cmp_pe, cmp_w1, cmp_w2)
    p3 = p.reshape(depth, n, PLE_DIM)

    h = x.reshape(n, D_MODEL)
    for i in range(depth):
        r3 = lambda a: a.reshape(bsz, seq, -1)
        (hq, lf, lk, hv, hgate, qn, qr, kcm, vcm, ksl, vsl, kwn, vwn, ng, nz) = _proj_call(
            h, row_vec(norm_g), w_all, cos_t, sa_t, sb_t, qg_all, kg_tiled, lbv, i, seq, tm)
        y_hg = _hgrn_call(r3(hq), r3(lf), r3(lk), r3(hv), r3(hgate), row_vec(hgrn_onorm_g), i, tb)
        kc, vc = _cmp_call(r3(kcm), r3(vcm), *cmp_params, kg_cmp, i)
        y_nsa = _nsa_call(r3(qn), r3(qr), kc, vc, r3(ksl), r3(vsl), r3(kwn), r3(vwn), r3(ng), r3(nz), ovt, gexp,
                          tq, tk)
        h = _out_call(h, y_hg.reshape(n, HG_W), y_nsa.reshape(n, NSA_W), p3, wo16, row_vec(ple_norm_g), wpg16, wpp16,
                      i, tm)
    return h.reshape(bsz, seq, D_MODEL)
```

```python
import functools

import jax
import jax.numpy as jnp
from jax import lax
from jax.experimental import pallas as pl
from jax.experimental.pallas import tpu as pltpu

F32 = jnp.float32
BF16 = jnp.bfloat16

D_MODEL = 1024
HG_HEADS = 4
HG_DK = 128
HG_DV = 128
HG_W = HG_HEADS * HG_DV
HG_CHUNK = 64
HG_SUB = 16
NSA_HEADS = 8
NSA_KV_HEADS = 2
NSA_DH = 64
NSA_G = NSA_HEADS // NSA_KV_HEADS
NSA_W = NSA_HEADS * NSA_DH
KV_W = NSA_KV_HEADS * NSA_DH
CMP_BLOCK = 32
CMP_STRIDE = 16
CMP_HIDDEN = 128
SLC_BLOCK = 64
SLC_TOPK = 16
WINDOW = 512
FORCE_SCORE = 1e4
ROPE_THETA = 500000.0
ROPE_DIM = NSA_DH // 4
PLE_DIM = 256
RMS_EPS = 1e-6
IN_SPLITS = (HG_W,) * 5 + (NSA_W,) + (KV_W,) * 6 + (3 * NSA_HEADS, NSA_W)

LANE = 128
SUBLANES = 8
NEG = -1e30
LOG2E = 1.4426950408889634
SLC_SHIFT = SLC_BLOCK.bit_length() - 1
MXU_COLS = 256
PENALTY = -(2.0 ** 100)
RANK_STAGES = 4
VMEM_LIMIT = 56 * 1024 * 1024

V_SLOT = LANE

C_HG = 0
C_Q = C_HG + 5 * HG_W
C_KCM = C_Q + NSA_W
C_KSL = C_KCM + 2 * KV_W
C_KWN = C_KSL + 2 * KV_W
C_NG = C_KWN + 2 * KV_W
C_NZ = C_NG + LANE
C_END = C_NZ + NSA_W


def _dot(a, b):
    return jnp.dot(a, b, preferred_element_type=F32)


def _dot_nt(a, b):
    return lax.dot_general(a, b, (((1,), (1,)), ((), ())), preferred_element_type=F32)


def _dot_tn(a, b):
    return lax.dot_general(a, b, (((0,), (0,)), ((), ())), preferred_element_type=F32)


def _sigmoid(x):
    return 1.0 / (1.0 + jnp.exp(-x))


def _silu(x):
    return x * _sigmoid(x)


def _slot_norm(x, g):
    ms = jnp.sum(x * x, axis=-1, keepdims=True) * (1.0 / NSA_DH)
    return x * lax.rsqrt(ms + RMS_EPS) * g


def _proj_kernel(h_ref, g_ref, w_ref, cos_ref, sa_ref, sb_ref, qg_ref, kg_ref, lb_ref,
                 hq_ref, lf_ref, lk_ref, hv_ref, hgate_ref, qn_ref, qr_ref, kcm_ref, vcm_ref, ksl_ref, vsl_ref,
                 kwn_ref, vwn_ref, ng_ref, nz_ref, *, tiles_per_seq):
    x = h_ref[...]
    var = jnp.mean(x * x, axis=-1, keepdims=True)
    xn = (x * lax.rsqrt(var + RMS_EPS) * g_ref[...]).astype(BF16)

    def mm(a, b):
        return _dot(xn, w_ref[:, a:b])

    W = HG_W
    tm = x.shape[0]

    def decay_prep(fl, head):
        nrows, row0 = tm, 0
        cols = slice(head * LANE, (head + 1) * LANE)
        loglb, l1mlb = lb_ref[0:1, cols], lb_ref[1:2, cols]
        l1pe = jnp.log(1.0 + jnp.exp(-jnp.abs(fl)))
        cc = l1mlb + (jnp.minimum(fl, 0.0) - l1pe)
        lf = (jnp.maximum(loglb, cc) + jnp.log(1.0 + jnp.exp(-jnp.abs(loglb - cc)))) * LOG2E
        lk = (l1mlb - jnp.maximum(fl, 0.0) - l1pe) * LOG2E
        t3 = lf.reshape(nrows // SUBLANES, SUBLANES, LANE)
        sub = lax.broadcasted_iota(jnp.int32, t3.shape, 1)
        for step in (1, 2, 4):
            t3 = t3 + jnp.where(sub >= step, pltpu.roll(t3, step, 1), 0.0)
        per_chunk = HG_CHUNK // SUBLANES
        for tile in range(nrows // SUBLANES):
            rows = slice(tile * SUBLANES, (tile + 1) * SUBLANES)
            out_rows = slice(row0 + tile * SUBLANES, row0 + (tile + 1) * SUBLANES)
            total = t3[tile, SUBLANES - 1:SUBLANES, :]
            if tile % per_chunk == 0:
                b, run = t3[tile], total
            else:
                b, run = t3[tile] + run, run + total
            lf_ref[out_rows, cols] = b
            lk_ref[out_rows, cols] = b - lk[rows]

    fl_a = mm(C_HG + W, C_HG + W + W // 2)
    qs = mm(C_Q, C_KCM)
    decay_prep(fl_a[:, :LANE], 0)
    kvc = mm(C_KCM, C_KSL)
    kv_s = mm(C_KSL, C_KWN)
    decay_prep(fl_a[:, LANE:], 1)
    fl_b = mm(C_HG + W + W // 2, C_HG + 2 * W)
    kv_w = mm(C_KWN, C_NG)
    gz = mm(C_NG, C_END)
    decay_prep(fl_b[:, :LANE], 2)
    decay_prep(fl_b[:, LANE:], 3)

    c, sa, sb = cos_ref[...], sa_ref[...], sb_ref[...]
    scale = NSA_DH ** -0.5 * LOG2E
    lane = lax.broadcasted_iota(jnp.int32, (tm, LANE), 1)
    low = lane < NSA_DH
    kcm_ref[...] = kvc[:, :KV_W]
    vcm_ref[...] = kvc[:, KV_W:]
    ng_ref[...] = gz[:, :LANE]
    nz_ref[...] = gz[:, LANE:]

    nq = NSA_HEADS // 2
    tiles = [qs[:, pr * LANE:(pr + 1) * LANE] for pr in range(nq)] + [kv_s[:, :KV_W], kv_w[:, :KV_W]]
    gains = [qg_ref[...] * scale] * nq + [kg_ref[0:1, :], kg_ref[1:2, :]]
    half = len(tiles) // 2
    sq = jnp.concatenate([jnp.concatenate([t * t for t in tiles[:half]], axis=0),
                          jnp.concatenate([t * t for t in tiles[half:]], axis=0)], axis=1)
    same_head = (jnp.right_shift(lax.broadcasted_iota(jnp.int32, (2 * LANE, 2 * LANE), 0), NSA_DH.bit_length() - 1)
                 == jnp.right_shift(lax.broadcasted_iota(jnp.int32, (2 * LANE, 2 * LANE), 1), NSA_DH.bit_length() - 1))
    head_sum = jnp.where(same_head, 1.0, 0.0).astype(BF16)
    sq_hi = sq.astype(BF16)
    sq_lo = (sq - sq_hi.astype(F32)).astype(BF16)
    ssq = _dot(sq_hi, head_sum) + _dot(sq_lo, head_sum)
    ms = [ssq[(j % half) * tm:(j % half + 1) * tm, (j // half) * LANE:(j // half + 1) * LANE] * (1.0 / NSA_DH)
          for j in range(len(tiles))]
    normed = [t * lax.rsqrt(m + RMS_EPS) * g for t, m, g in zip(tiles, ms, gains)]
    hq_ref[...] = mm(C_HG, C_HG + W)
    roped = [t * c + pltpu.roll(t, LANE - ROPE_DIM // 2, 1) * sa + pltpu.roll(t, ROPE_DIM // 2, 1) * sb for t in normed]
    hv_ref[...] = mm(C_HG + 2 * W, C_HG + 3 * W).astype(BF16)

    def head_slots(tile):
        return jnp.where(low, tile, 0.0), jnp.where(low, pltpu.roll(tile, NSA_DH, 1), 0.0)

    pos = (pl.program_id(0) % tiles_per_seq) * tm + lax.broadcasted_iota(jnp.int32, (tm, LANE), 0)
    blk_onehot = jnp.where(lane - NSA_DH == jnp.right_shift(pos, SLC_SHIFT), 1.0, 0.0)
    for pr in range(nq):
        for k, (a, b) in enumerate(zip(head_slots(normed[pr]), head_slots(roped[pr]))):
            sl = slice((2 * pr + k) * LANE, (2 * pr + k + 1) * LANE)
            qn_ref[:, sl] = a.astype(BF16)
            qr_ref[:, sl] = b.astype(BF16)
    for kv, (ks, kw) in enumerate(zip(head_slots(roped[nq]), head_slots(roped[nq + 1]))):
        sl = slice(kv * LANE, (kv + 1) * LANE)
        ksl_ref[:, sl] = jnp.where(low, ks, blk_onehot).astype(BF16)
        kwn_ref[:, sl] = kw.astype(BF16)
    for ref, t in ((vsl_ref, kv_s[:, KV_W:]), (vwn_ref, kv_w[:, KV_W:])):
        for kv, v in enumerate((t, pltpu.roll(t, NSA_DH, 1))):
            ref[:, kv * V_SLOT:(kv + 1) * V_SLOT] = jnp.where(low, v, 1.0).astype(BF16)

    hgate_ref[...] = _sigmoid(mm(C_HG + 3 * W, C_HG + 4 * W)) * _silu(mm(C_HG + 4 * W, C_HG + 5 * W))


def _of_layer(a, layer, **kw):
    return pl.BlockSpec((None,) + a.shape[1:], lambda *_: (layer,) + (0,) * (a.ndim - 1), **kw)


def _proj_call(h2, g, w, cos_t, sa_t, sb_t, qg, kg, lbv, layer, seq, tm):
    n = h2.shape[0]
    tpb = seq // tm
    row = lambda width: pl.BlockSpec((tm, width), lambda i: (i, 0))
    tab = pl.BlockSpec((tm, LANE), lambda i: (i % tpb, 0))
    k_slots, v_slots = NSA_KV_HEADS * LANE, NSA_KV_HEADS * V_SLOT
    out_w = [(HG_W, F32), (HG_W, F32), (HG_W, F32), (HG_W, BF16), (HG_W, F32),
             (NSA_HEADS * LANE, BF16), (NSA_HEADS * LANE, BF16), (KV_W, F32), (KV_W, F32),
             (k_slots, BF16), (v_slots, BF16), (k_slots, BF16), (v_slots, BF16), (LANE, F32), (NSA_W, F32)]
    return pl.pallas_call(
        functools.partial(_proj_kernel, tiles_per_seq=tpb),
        grid=(n // tm,),
        in_specs=[row(D_MODEL), _of_layer(g, layer), _of_layer(w, layer, pipeline_mode=pl.Buffered(1)), tab, tab, tab,
                  _of_layer(qg, layer), _of_layer(kg, layer), _of_layer(lbv, layer)],
        out_specs=[row(wd) for wd, _ in out_w],
        out_shape=[jax.ShapeDtypeStruct((n, wd), dt) for wd, dt in out_w],
        compiler_params=pltpu.CompilerParams(dimension_semantics=("parallel",), vmem_limit_bytes=VMEM_LIMIT),
        name="in_proj",
    )(h2, g, w, cos_t, sa_t, sb_t, qg, kg, lbv)


def _hgrn_kernel(q_ref, b_ref, c_ref, v_ref, gate_ref, og_ref, y_ref, st_ref, u_ref, *, tb):
    C, SUB, HALF = HG_CHUNK, HG_SUB, HG_SUB // 2
    nch, nblk = tb // C, C // SUB

    @pl.when(pl.program_id(2) == 0)
    def _():
        st_ref[...] = jnp.zeros_like(st_ref)

    og = og_ref[...]
    lane8 = lax.broadcasted_iota(jnp.int32, (HALF, LANE), 1)
    lane_c = lax.broadcasted_iota(jnp.int32, (SUB, C), 1)
    row_c = lax.broadcasted_iota(jnp.int32, (SUB, C), 0)
    left_col = jnp.where(lax.broadcasted_iota(jnp.int32, (2 * HG_DK, 2 * LANE), 1) < LANE, 1.0, 0.0)
    sum_w = jnp.where(lax.broadcasted_iota(jnp.int32, (2 * HG_DK, 2 * LANE), 0) < HG_DK,
                      left_col, 1.0 - left_col).astype(BF16)

    q = [q_ref[k * C:(k + 1) * C, :] for k in range(nch)]
    b = [b_ref[k * C:(k + 1) * C, :] for k in range(nch)]
    c = [c_ref[k * C:(k + 1) * C, :] for k in range(nch)]
    v16 = [v_ref[k * C:(k + 1) * C, :] for k in range(nch)]

    left, right = [], []
    for k in range(nch):
        for blk in range(nblk):
            r0 = blk * SUB
            q_t, q_b = q[k][r0:r0 + HALF], q[k][r0 + HALF:r0 + SUB]
            b_t, b_b = b[k][r0:r0 + HALF], b[k][r0 + HALF:r0 + SUB]
            tops = [q_t * jnp.exp2(b_t - c[k][r0 + s:r0 + s + 1, :]) for s in range(HALF)]
            bots = [q_b * jnp.exp2(b_b - c[k][r0 + s:r0 + s + 1, :]) for s in range(SUB)]
            left += tops + bots[:HALF // 2]
            right += bots[HALF // 2:]
    per_blk = HALF + HALF // 2
    xs = jnp.concatenate([jnp.concatenate(left, axis=0), jnp.concatenate(right, axis=0)], axis=1)
    rs = _dot(xs.astype(BF16), sum_w)

    a = []
    for k in range(nch):
        a_rows = []
        for blk in range(nblk):
            r0 = blk * SUB
            base = (k * nblk + blk) * per_blk * HALF

            def piece(idx, side):
                return rs[base + idx * HALF:base + (idx + 1) * HALF, side * LANE:(side + 1) * LANE]

            a_t = jnp.zeros((HALF, LANE), F32)
            a_b = jnp.zeros((HALF, LANE), F32)
            for s in range(HALF):
                a_t = jnp.where(lane8 == s, piece(s, 0), a_t)
            for s in range(SUB):
                src = piece(HALF + s, 0) if s < HALF // 2 else piece(s - HALF // 2, 1)
                a_b = jnp.where(lane8 == s, src, a_b)
            a_d = jnp.concatenate([a_t, a_b], axis=0)
            if blk > 0:
                ref_b = b[k][r0 - 1:r0, :]
                q_p = (q[k][r0:r0 + SUB] * jnp.exp2(b[k][r0:r0 + SUB] - ref_b)).astype(BF16)
                k_p = jnp.concatenate([jnp.exp2(ref_b - c[k][:r0]), jnp.zeros((C - r0, HG_DK), F32)], axis=0)
                a_i = pltpu.roll(a_d, r0, 1)[:, :C] + _dot_nt(q_p, k_p.astype(BF16))
            else:
                a_i = a_d[:, :C]
            a_rows.append(jnp.where(lane_c <= r0 + row_c, a_i, 0.0))
        a.append(jnp.concatenate(a_rows, axis=0).astype(BF16))

    for k in range(nch):
        u_ref[k] = _dot_tn(v16[k], jnp.exp2(b[k][C - 1:C, :] - c[k]).astype(BF16))

    st = st_ref[...]
    for k in range(nch):
        o = _dot(a[k], v16[k]) + _dot_nt((q[k] * jnp.exp2(b[k])).astype(BF16), st.astype(BF16))
        st = st * jnp.exp2(b[k][C - 1:C, :]) + u_ref[k]
        ms = jnp.mean(o * o, axis=-1, keepdims=True)
        y_ref[k * C:(k + 1) * C, :] = o * lax.rsqrt(ms + RMS_EPS) * og * gate_ref[k * C:(k + 1) * C, :]
    st_ref[...] = st


def _hgrn_call(hq, lf, lk, hv, hgate, og, layer, tb):
    bsz, seq, _ = hq.shape
    piece = pl.BlockSpec((None, tb, HG_DK), lambda b, h, i: (b, i, h))
    return pl.pallas_call(
        functools.partial(_hgrn_kernel, tb=tb),
        grid=(bsz, HG_HEADS, seq // tb),
        in_specs=[piece, piece, piece, piece, piece, _of_layer(og, layer)],
        out_specs=piece,
        out_shape=jax.ShapeDtypeStruct((bsz, seq, HG_W), F32),
        scratch_shapes=[pltpu.VMEM((HG_DV, HG_DK), F32), pltpu.VMEM((tb // HG_CHUNK, HG_DV, HG_DK), F32)],
        compiler_params=pltpu.CompilerParams(dimension_semantics=("parallel", "parallel", "arbitrary"),
                                             vmem_limit_bytes=VMEM_LIMIT),
        name="hgrn2",
    )(hq, lf, lk, hv, hgate, og)


def _cmp_kernel(kcm_ref, vcm_ref, pe_ref, w1_ref, w2_ref, g_ref, kc_ref, vc_ref):
    half = CMP_BLOCK // 2
    ncp = kcm_ref.shape[0] // CMP_STRIDE
    for which, (src, out_ref) in enumerate(((kcm_ref, kc_ref), (vcm_ref, vc_ref))):
        first = jnp.zeros((ncp, NSA_KV_HEADS * CMP_HIDDEN), F32)
        second = jnp.zeros((ncp, NSA_KV_HEADS * CMP_HIDDEN), F32)
        for r in range(half):
            tok = src[pl.ds(r, ncp, stride=CMP_STRIDE), :]
            first = first + _dot((tok + pe_ref[which, r:r + 1, :]).astype(BF16), w1_ref[which, r])
            second = second + _dot((tok + pe_ref[which, half + r:half + r + 1, :]).astype(BF16), w1_ref[which, half + r])
        hid = first + pltpu.roll(second, ncp - 1, 0)
        out = _dot(_silu(hid).astype(BF16), w2_ref[which])
        if which == 0:
            for kv in range(NSA_KV_HEADS):
                sl = slice(kv * LANE, (kv + 1) * LANE)
                out_ref[:, sl] = _slot_norm(out[:, sl], g_ref[...]).astype(BF16)
        else:
            head_lane = jnp.bitwise_and(lax.broadcasted_iota(jnp.int32, out.shape, 1), LANE - 1)
            out_ref[...] = jnp.where(head_lane < NSA_DH, out, 1.0).astype(BF16)


def _cmp_call(kcm, vcm, pe, w1, w2, g, layer):
    bsz, seq, _ = kcm.shape
    ncp = seq // CMP_STRIDE
    slots = NSA_KV_HEADS * LANE
    tok = pl.BlockSpec((None, seq, KV_W), lambda b: (b, 0, 0))
    out = pl.BlockSpec((None, ncp, slots), lambda b: (b, 0, 0))
    return pl.pallas_call(
        _cmp_kernel,
        grid=(bsz,),
        in_specs=[tok, tok, _of_layer(pe, layer), _of_layer(w1, layer), _of_layer(w2, layer), _of_layer(g, layer)],
        out_specs=[out, out],
        out_shape=[jax.ShapeDtypeStruct((bsz, ncp, slots), BF16)] * 2,
        compiler_params=pltpu.CompilerParams(dimension_semantics=("parallel",), vmem_limit_bytes=VMEM_LIMIT),
        name="compress",
    )(kcm, vcm, pe, w1, w2, g)


def _attend(q, k, v, bias, acc_ref, m_ref):
    p, m_new = _exp_tiles(_score_tiles(q, k, bias), m_ref[...])
    acc_ref[...] = jnp.exp2(m_ref[...] - m_new) * acc_ref[...] + _pv_tiles(p, v)
    m_ref[...] = m_new


def _score_tiles(q, k, bias):
    pieces = []
    for c in range(k.shape[0] // MXU_COLS):
        s = _dot_nt(q, k[c * MXU_COLS:(c + 1) * MXU_COLS])
        pieces.append(s if bias is None else s + bias[:, c * MXU_COLS:(c + 1) * MXU_COLS])
    return pieces


def _exp_tiles(pieces, m_floor):
    row_max = functools.reduce(jnp.maximum, [jnp.max(s, axis=-1, keepdims=True) for s in pieces])
    m = row_max if m_floor is None else jnp.maximum(m_floor, row_max)
    m_wide = jnp.concatenate([jnp.broadcast_to(m, (pieces[0].shape[0], LANE))] * (MXU_COLS // LANE), axis=1)
    return [jnp.exp2(s - m_wide).astype(BF16) for s in pieces], m


def _pv_tiles(p, v):
    return functools.reduce(lambda a, b: a + b,
                            [_dot(pc, v[c * MXU_COLS:(c + 1) * MXU_COLS]) for c, pc in enumerate(p)])


def _nsa_kernel(qn_ref, qr_ref, kc_ref, vc_ref, ksl_ref, vsl_ref, kwn_ref, vwn_ref, ng_ref, nz_ref,
                ovt_ref, gexp_ref, y_ref, qa_ref, ocmp_ref, acc_s_ref, m_s_ref, acc_w_ref, *, tq, tk):
    G = NSA_G
    R = G * tq
    i = pl.program_id(2)
    t0 = i * tq
    ncp = kc_ref.shape[0]
    nsb = ovt_ref.shape[0] - 8
    topk = min(SLC_TOPK, nsb)
    lane = lax.broadcasted_iota(jnp.int32, (tq, LANE), 1)

    def heads_on_rows(ref):
        return jnp.concatenate([ref[:, g * LANE:(g + 1) * LANE] for g in range(G)], axis=0)

    def all_heads(a):
        return jnp.concatenate([a] * G, axis=0)

    q_pos = t0 + lax.broadcasted_iota(jnp.int32, (tq, 1), 0)

    def position_bias(start, width, window=None):
        dist = q_pos - (start + lax.broadcasted_iota(jnp.int32, (tq, width), 1))
        ok = jnp.where(dist >= 0, 0.0, NEG)
        return all_heads(ok if window is None else jnp.where(dist < window, ok, NEG))

    def _compress_and_select(qn_src, qr_src, base):
        pos_col = base + lax.broadcasted_iota(jnp.int32, (tq, 1), 0)
        c_end = lax.broadcasted_iota(jnp.int32, (tq, ncp), 1) * CMP_STRIDE + (CMP_BLOCK - 1)
        s = _dot_nt(heads_on_rows(qn_src), kc_ref[...]) + all_heads(jnp.where(c_end <= pos_col, 0.0, NEG))
        e = jnp.exp2(s - jnp.max(s, axis=-1, keepdims=True))
        e_hi = e.astype(BF16)
        e_lo = (e - e_hi.astype(F32)).astype(BF16)
        ocmp_ref[...] = _dot(e_hi, vc_ref[...])

        ovt = ovt_ref[...]
        raw_t = _dot_nt(ovt, e_hi) + _dot_nt(ovt, e_lo)
        any_valid_t = jnp.where(base + lax.broadcasted_iota(jnp.int32, (1, tq), 1) >= CMP_BLOCK - 1, 1.0, 0.0)
        imp_t = jnp.zeros((nsb, tq), F32)
        for g in range(G):
            imp_t = imp_t + raw_t[:nsb, g * tq:(g + 1) * tq] * (any_valid_t / raw_t[nsb:nsb + 1, g * tq:(g + 1) * tq])
        j_row = lax.broadcasted_iota(jnp.int32, (nsb, tq), 0)
        cur = jnp.right_shift(base + lax.broadcasted_iota(jnp.int32, (nsb, tq), 1), SLC_SHIFT)
        forced = (j_row == 0) | (j_row == cur) | (j_row == cur - 1)
        score = jnp.where(forced, FORCE_SCORE, jnp.where(j_row <= cur, imp_t, -1.0))
        yield
        ngrp = nsb // 8
        sub8 = lax.broadcasted_iota(jnp.int32, (8, LANE), 0)
        pen_cols, steps = [], 0
        for qt in range(tq // LANE):
            sc = score[:, qt * LANE:(qt + 1) * LANE]
            grp = [sc[8 * a:8 * a + 8] for a in range(ngrp)]
            beats = [jnp.zeros((8, LANE), F32) for _ in range(ngrp)]
            for jp in range(nsb):
                row = sc[jp:jp + 1, :]
                for a in range(ngrp):
                    if a > jp // 8:
                        hit = jnp.where(row >= grp[a], 1.0, 0.0)
                    elif a < jp // 8:
                        hit = jnp.where(row > grp[a], 1.0, 0.0)
                    else:
                        hit = jnp.where(sub8 > jp % 8, jnp.where(row >= grp[a], 1.0, 0.0),
                                        jnp.where(row > grp[a], 1.0, 0.0))
                    beats[a] = beats[a] + hit
                steps += 1
                if steps % (nsb * (tq // LANE) // RANK_STAGES) == 0 and steps < nsb * (tq // LANE):
                    yield
            pen_cols.append(jnp.concatenate([jnp.where(b < topk, 0.0, PENALTY) for b in beats], axis=0))
        pen_t = jnp.concatenate(pen_cols, axis=1)
        zeros_t = jnp.zeros((NSA_DH, tq), F32)
        pad_t = [zeros_t[:LANE - NSA_DH - nsb]] if nsb < LANE - NSA_DH else []
        pen = jnp.concatenate([zeros_t, pen_t] + pad_t, axis=0).T.astype(BF16)
        for g in range(G):
            qa_ref[g * tq:(g + 1) * tq, :] = jnp.where(lane < NSA_DH, qr_src[:, g * LANE:(g + 1) * LANE], pen)

    select = _compress_and_select(qn_ref, qr_ref, t0)

    q4 = heads_on_rows(qr_ref)
    w0 = pl.multiple_of(jnp.maximum(t0 - 2 * tq, 0), tq)
    s_w = _score_tiles(q4, kwn_ref[pl.ds(w0, 3 * tq), :], position_bias(w0, 3 * tq, WINDOW))
    next(select)
    p_w, _ = _exp_tiles(s_w, None)
    next(select)
    acc_w_ref[...] = _pv_tiles(p_w, vwn_ref[pl.ds(w0, 3 * tq), :])
    for _ in select:
        pass

    qa = qa_ref[...]
    top = pl.multiple_of((i // 2) * tk, tk)
    p_top, m_top = _exp_tiles(_score_tiles(qa, ksl_ref[pl.ds(top, tk), :], position_bias(top, tk)), None)
    acc_s_ref[...] = _pv_tiles(p_top, vsl_ref[pl.ds(top, tk), :])
    m_s_ref[...] = jnp.broadcast_to(m_top, (R, LANE))

    def tiles_below(first, count):
        for c in range(count):
            r = pl.multiple_of((first + c) * tk, tk)
            _attend(qa, ksl_ref[pl.ds(r, tk), :], vsl_ref[pl.ds(r, tk), :], None, acc_s_ref, m_s_ref)

    n_below = i // 2
    lax.fori_loop(0, n_below // 4, lambda j, carry: tiles_below(4 * j, 4) or carry, 0)
    lax.fori_loop(0, (n_below % 4) // 2, lambda j, carry: tiles_below(n_below // 4 * 4, 2) or carry, 0)
    lax.fori_loop(0, n_below % 2, lambda j, carry: tiles_below(n_below - 1, 1) or carry, 0)

    sig = _sigmoid(ng_ref[...])
    sig_hi = sig.astype(BF16)
    sig_lo = (sig - sig_hi.astype(F32)).astype(BF16)
    gates = _dot(sig_hi, gexp_ref[...]) + _dot(sig_lo, gexp_ref[...])
    low = lane < NSA_DH
    any_valid = jnp.where(t0 + lax.broadcasted_iota(jnp.int32, (tq, LANE), 0) >= CMP_BLOCK - 1, 1.0, 0.0)
    o_cmp = ocmp_ref
    for pair in range(G // 2):
        ev = slice(2 * pair * tq, (2 * pair + 1) * tq)
        od = slice((2 * pair + 1) * tq, (2 * pair + 2) * tq)

        def gate(branch):
            lo = (branch * (G // 2) + pair) * LANE
            return gates[:, lo:lo + LANE]

        def normalised(acc):
            even, odd = acc[ev, :], acc[od, :]
            even_sw, odd_sw = pltpu.roll(even, NSA_DH, 1), pltpu.roll(odd, NSA_DH, 1)
            return jnp.where(low, even, odd_sw) / jnp.where(low, even_sw, odd)

        o = (gate(0) * any_valid * normalised(o_cmp) + gate(1) * normalised(acc_s_ref)
             + gate(2) * normalised(acc_w_ref))
        y_ref[:, pair * LANE:(pair + 1) * LANE] = o * _silu(nz_ref[:, pair * LANE:(pair + 1) * LANE])


def _nsa_call(qn, qr, kc, vc, ksl, vsl, kwn, vwn, ng, nz, ovt, gexp, tq, tk):
    bsz, seq, _ = qn.shape
    ncp = kc.shape[1]
    rows = NSA_G * tq
    per_kv = lambda nrows, width: pl.BlockSpec((None, nrows, width), lambda b, k, i: (b, 0, k))
    qspec = pl.BlockSpec((None, tq, NSA_G * LANE), lambda b, k, i: (b, i, k))
    yspec = pl.BlockSpec((None, tq, NSA_G * NSA_DH), lambda b, k, i: (b, i, k))
    return pl.pallas_call(
        functools.partial(_nsa_kernel, tq=tq, tk=tk),
        grid=(bsz, NSA_KV_HEADS, seq // tq),
        in_specs=[qspec, qspec, per_kv(ncp, LANE), per_kv(ncp, V_SLOT), per_kv(seq, LANE),
                  per_kv(seq, V_SLOT), per_kv(seq, LANE), per_kv(seq, V_SLOT),
                  pl.BlockSpec((None, tq, LANE), lambda b, k, i: (b, i, 0)), yspec,
                  pl.BlockSpec(ovt.shape, lambda b, k, i: (0, 0)),
                  pl.BlockSpec((None,) + gexp.shape[1:], lambda b, k, i: (k, 0, 0))],
        out_specs=yspec,
        out_shape=jax.ShapeDtypeStruct((bsz, seq, NSA_W), F32),
        scratch_shapes=[pltpu.VMEM((rows, LANE), BF16), pltpu.VMEM((rows, LANE), F32),
                        pltpu.VMEM((rows, V_SLOT), F32), pltpu.VMEM((rows, LANE), F32), pltpu.VMEM((rows, V_SLOT), F32)],
        compiler_params=pltpu.CompilerParams(dimension_semantics=("parallel", "parallel", "parallel"),
                                             vmem_limit_bytes=VMEM_LIMIT),
        name="nsa_attn",
    )(qn, qr, kc, vc, ksl, vsl, kwn, vwn, ng, nz, ovt, gexp)


def _out_kernel(h_ref, yh_ref, yn_ref, p_ref, wo_ref, g_ref, wpg_ref, wpp_ref, o_ref):
    h = h_ref[...] + _dot(yh_ref[...].astype(BF16), wo_ref[0:HG_W, :]) + _dot(yn_ref[...].astype(BF16), wo_ref[HG_W:, :])
    var = jnp.mean(h * h, axis=-1, keepdims=True)
    hn = (h * lax.rsqrt(var + RMS_EPS) * g_ref[...]).astype(BF16)
    gate = _sigmoid(_dot(hn, wpg_ref[...]))
    o_ref[...] = h + gate * _dot(p_ref[...].astype(BF16), wpp_ref[...])


def _out_call(h2, yh, yn, p3, wo, g, wpg, wpp, layer, tm):
    n = h2.shape[0]
    row = lambda width: pl.BlockSpec((tm, width), lambda i: (i, 0))
    return pl.pallas_call(
        _out_kernel,
        grid=(n // tm,),
        in_specs=[row(D_MODEL), row(HG_W), row(NSA_W), pl.BlockSpec((None, tm, PLE_DIM), lambda i: (layer, i, 0)),
                  _of_layer(wo, layer), _of_layer(g, layer), _of_layer(wpg, layer), _of_layer(wpp, layer)],
        out_specs=row(D_MODEL),
        out_shape=jax.ShapeDtypeStruct((n, D_MODEL), F32),
        compiler_params=pltpu.CompilerParams(dimension_semantics=("parallel",), vmem_limit_bytes=VMEM_LIMIT),
        name="out_proj_ple",
    )(h2, yh, yn, p3, wo, g, wpg, wpp)


def _pad_last(a, width):
    return jnp.pad(a, [(0, 0)] * (a.ndim - 1) + [(0, width - a.shape[-1])])


def _layout_w_in(w_in):
    n_gate = 3 * NSA_HEADS
    body, ngate, nz = w_in[..., :C_NG], w_in[..., C_NG:C_NG + n_gate], w_in[..., C_NG + n_gate:]
    return jnp.concatenate([body, _pad_last(ngate, LANE), nz], axis=-1).astype(BF16)


def _layout_cmp(pe, w1, w2):
    kvh = NSA_KV_HEADS
    lead = w1.shape[:2]
    eye = jnp.eye(kvh, dtype=F32)
    w1r = w1.reshape(lead + (CMP_BLOCK, NSA_DH, CMP_HIDDEN))
    w1x = jnp.einsum('nwldh,gk->nwlgdkh', w1r, eye).reshape(lead + (CMP_BLOCK, kvh * NSA_DH, kvh * CMP_HIDDEN))
    w2x = jnp.einsum('nwhs,gk->nwghks', _pad_last(w2, LANE), eye).reshape(lead + (kvh * CMP_HIDDEN, kvh * LANE))
    pe2 = jnp.tile(pe, (1, 1, 1, kvh))
    return pe2, w1x.astype(BF16), w2x.astype(BF16)


def kernel(x, p, norm_g, w_in, hgrn_lb, hgrn_onorm_g, nsa_qnorm_g, nsa_knorm_g, cmp_pe, cmp_w1, cmp_w2, w_out,
           ple_norm_g, w_pg, w_pp):
    bsz, seq, _ = x.shape
    depth = w_in.shape[0]
    n = bsz * seq
    tm = 256
    tq, tk, tb = WINDOW // 2, WINDOW, 1024
    assert seq % tk == 0 and seq // SLC_BLOCK <= LANE - NSA_DH and (seq // SLC_BLOCK) % 8 == 0
    ncp = seq // CMP_STRIDE
    nsb = seq // SLC_BLOCK

    pos = jnp.arange(seq, dtype=F32)
    inv = ROPE_THETA ** (-jnp.arange(0, ROPE_DIM, 2, dtype=F32) / ROPE_DIM)
    ang = pos[:, None] * inv[None, :]
    cos, sin = jnp.cos(ang), jnp.sin(ang)
    half = ROPE_DIM // 2
    per_head = lambda t: jnp.tile(_pad_last(t, NSA_DH), (1, LANE // NSA_DH))
    cos_t = per_head(jnp.concatenate([cos, cos, jnp.ones((seq, NSA_DH - ROPE_DIM), F32)], axis=1))
    sa_t = per_head(-sin)
    sb_t = per_head(jnp.concatenate([jnp.zeros((seq, half), F32), sin], axis=1))

    lb_all = jnp.cumsum(jax.nn.softmax(hgrn_lb.astype(F32), axis=0), axis=0)
    lb_all = lb_all - lb_all[0]
    lbv = jnp.stack([jnp.log(lb_all), jnp.log1p(-lb_all)], axis=1)

    c_tok = jnp.arange(ncp)[:, None] * CMP_STRIDE + jnp.arange(CMP_BLOCK)[None, :]
    overlap = jnp.mean((c_tok[..., None] // SLC_BLOCK == jnp.arange(nsb)).astype(F32), axis=1)
    ovt = jnp.concatenate([overlap.T, jnp.ones((8, ncp), F32)], axis=0).astype(BF16)
    col = jnp.arange(3 * (NSA_G // 2) * LANE)
    src = (col // (NSA_G // 2 * LANE)) * NSA_HEADS + ((col // LANE) % (NSA_G // 2)) * 2 + (col % LANE) // NSA_DH
    gexp = (jnp.arange(LANE)[None, :, None] == src[None, None, :] + NSA_G * jnp.arange(NSA_KV_HEADS)[:, None, None]).astype(BF16)

    w_all = _layout_w_in(w_in)
    wo16, wpg16, wpp16 = w_out.astype(BF16), w_pg.astype(BF16), w_pp.astype(BF16)
    row_vec = lambda a: a[:, None, :]
    qg_all = row_vec(jnp.tile(nsa_qnorm_g, (1, LANE // NSA_DH)))
    kg_tiled = jnp.tile(nsa_knorm_g[:, 1:], (1, 1, LANE // NSA_DH))
    kg_cmp = _pad_last(nsa_knorm_g[:, 0:1], LANE)
    cmp_params = _layout_cmp(cmp_pe, cmp_w1, cmp_w2)
    p3 = p.reshape(depth, n, PLE_DIM)

    h = x.reshape(n, D_MODEL)
    for i in range(depth):
        r3 = lambda a: a.reshape(bsz, seq, -1)
        (hq, lf, lk, hv, hgate, qn, qr, kcm, vcm, ksl, vsl, kwn, vwn, ng, nz) = _proj_call(
            h, row_vec(norm_g), w_all, cos_t, sa_t, sb_t, qg_all, kg_tiled, lbv, i, seq, tm)
        y_hg = _hgrn_call(r3(hq), r3(lf), r3(lk), r3(hv), r3(hgate), row_vec(hgrn_onorm_g), i, tb)
        kc, vc = _cmp_call(r3(kcm), r3(vcm), *cmp_params, kg_cmp, i)
        y_nsa = _nsa_call(r3(qn), r3(qr), kc, vc, r3(ksl), r3(vsl), r3(kwn), r3(vwn), r3(ng), r3(nz), ovt, gexp,
                          tq, tk)
        h = _out_call(h, y_hg.reshape(n, HG_W), y_nsa.reshape(n, NSA_W), p3, wo16, row_vec(ple_norm_g), wpg16, wpp16,
                      i, tm)
    return h.reshape(bsz, seq, D_MODEL)
```

```python
import functools

import jax
import jax.numpy as jnp
from jax import lax
from jax.experimental import pallas as pl
from jax.experimental.pallas import tpu as pltpu

F32 = jnp.float32
BF16 = jnp.bfloat16

D_MODEL = 1024
HG_HEADS = 4
HG_DK = 128
HG_DV = 128
HG_W = HG_HEADS * HG_DV
HG_CHUNK = 64
HG_SUB = 16
NSA_HEADS = 8
NSA_KV_HEADS = 2
NSA_DH = 64
NSA_G = NSA_HEADS // NSA_KV_HEADS
NSA_W = NSA_HEADS * NSA_DH
KV_W = NSA_KV_HEADS * NSA_DH
CMP_BLOCK = 32
CMP_STRIDE = 16
CMP_HIDDEN = 128
SLC_BLOCK = 64
SLC_TOPK = 16
WINDOW = 512
FORCE_SCORE = 1e4
ROPE_THETA = 500000.0
ROPE_DIM = NSA_DH // 4
PLE_DIM = 256
RMS_EPS = 1e-6
IN_SPLITS = (HG_W,) * 5 + (NSA_W,) + (KV_W,) * 6 + (3 * NSA_HEADS, NSA_W)

LANE = 128
SUBLANES = 8
NEG = -1e30
LOG2E = 1.4426950408889634
SLC_SHIFT = SLC_BLOCK.bit_length() - 1
MXU_COLS = 256
PENALTY = -(2.0 ** 100)
RANK_STAGES = 4
VMEM_LIMIT = 56 * 1024 * 1024

V_SLOT = LANE

C_HG = 0
C_Q = C_HG + 5 * HG_W
C_KCM = C_Q + NSA_W
C_KSL = C_KCM + 2 * KV_W
C_KWN = C_KSL + 2 * KV_W
C_NG = C_KWN + 2 * KV_W
C_NZ = C_NG + LANE
C_END = C_NZ + NSA_W


def _dot(a, b):
    return jnp.dot(a, b, preferred_element_type=F32)


def _dot_nt(a, b):
    return lax.dot_general(a, b, (((1,), (1,)), ((), ())), preferred_element_type=F32)


def _dot_tn(a, b):
    return lax.dot_general(a, b, (((0,), (0,)), ((), ())), preferred_element_type=F32)


def _sigmoid(x):
    return 1.0 / (1.0 + jnp.exp(-x))


def _silu(x):
    return x * _sigmoid(x)


def _slot_norm(x, g):
    ms = jnp.sum(x * x, axis=-1, keepdims=True) * (1.0 / NSA_DH)
    return x * lax.rsqrt(ms + RMS_EPS) * g


def _proj_kernel(h_ref, g_ref, w_ref, wtail_ref, cos_ref, sa_ref, sb_ref, qg_ref, kg_ref, lb_ref,
                 hq_ref, lf_ref, lk_ref, hv_ref, hgate_ref, qn_ref, qr_ref, kcm_ref, vcm_ref, ksl_ref, vsl_ref,
                 kwn_ref, vwn_ref, ng_ref, nz_ref, *, tiles_per_seq):
    x = h_ref[...]
    var = jnp.mean(x * x, axis=-1, keepdims=True)
    xn = (x * lax.rsqrt(var + RMS_EPS) * g_ref[...]).astype(BF16)

    def mm(a, b):
        return _dot(xn, w_ref[:, a:b])

    W = HG_W
    tm = x.shape[0]

    def decay_prep(fl, head):
        nrows, row0 = tm, 0
        cols = slice(head * LANE, (head + 1) * LANE)
        loglb, l1mlb = lb_ref[0:1, cols], lb_ref[1:2, cols]
        l1pe = jnp.log(1.0 + jnp.exp(-jnp.abs(fl)))
        cc = l1mlb + (jnp.minimum(fl, 0.0) - l1pe)
        lf = (jnp.maximum(loglb, cc) + jnp.log(1.0 + jnp.exp(-jnp.abs(loglb - cc)))) * LOG2E
        lk = (l1mlb - jnp.maximum(fl, 0.0) - l1pe) * LOG2E
        t3 = lf.reshape(nrows // SUBLANES, SUBLANES, LANE)
        sub = lax.broadcasted_iota(jnp.int32, t3.shape, 1)
        for step in (1, 2, 4):
            t3 = t3 + jnp.where(sub >= step, pltpu.roll(t3, step, 1), 0.0)
        per_chunk = HG_CHUNK // SUBLANES
        for tile in range(nrows // SUBLANES):
            rows = slice(tile * SUBLANES, (tile + 1) * SUBLANES)
            out_rows = slice(row0 + tile * SUBLANES, row0 + (tile + 1) * SUBLANES)
            total = t3[tile, SUBLANES - 1:SUBLANES, :]
            if tile % per_chunk == 0:
                b, run = t3[tile], total
            else:
                b, run = t3[tile] + run, run + total
            lf_ref[out_rows, cols] = b
            lk_ref[out_rows, cols] = b - lk[rows]

    fl_a = mm(C_HG + W, C_HG + W + W // 2)
    qs = mm(C_Q, C_KCM)
    decay_prep(fl_a[:, :LANE], 0)
    kvc = mm(C_KCM, C_KSL)
    kv_s = mm(C_KSL, C_KWN)
    decay_prep(fl_a[:, LANE:], 1)
    fl_b = mm(C_HG + W + W // 2, C_HG + 2 * W)
    kv_w = mm(C_KWN, C_NG)
    gz = _dot(xn, wtail_ref[...])
    decay_prep(fl_b[:, :LANE], 2)
    decay_prep(fl_b[:, LANE:], 3)

    c, sa, sb = cos_ref[...], sa_ref[...], sb_ref[...]
    scale = NSA_DH ** -0.5 * LOG2E
    lane = lax.broadcasted_iota(jnp.int32, (tm, LANE), 1)
    low = lane < NSA_DH
    kcm_ref[...] = kvc[:, :KV_W]
    vcm_ref[...] = kvc[:, KV_W:]
    ng_ref[...] = gz[:, :LANE]
    nz_ref[...] = gz[:, LANE:]

    nq = NSA_HEADS // 2
    tiles = [qs[:, pr * LANE:(pr + 1) * LANE] for pr in range(nq)] + [kv_s[:, :KV_W], kv_w[:, :KV_W]]
    gains = [qg_ref[...] * scale] * nq + [kg_ref[0:1, :], kg_ref[1:2, :]]
    half = len(tiles) // 2
    sq = jnp.concatenate([jnp.concatenate([t * t for t in tiles[:half]], axis=0),
                          jnp.concatenate([t * t for t in tiles[half:]], axis=0)], axis=1)
    same_head = (jnp.right_shift(lax.broadcasted_iota(jnp.int32, (2 * LANE, 2 * LANE), 0), NSA_DH.bit_length() - 1)
                 == jnp.right_shift(lax.broadcasted_iota(jnp.int32, (2 * LANE, 2 * LANE), 1), NSA_DH.bit_length() - 1))
    head_sum = jnp.where(same_head, 1.0, 0.0).astype(BF16)
    sq_hi = sq.astype(BF16)
    sq_lo = (sq - sq_hi.astype(F32)).astype(BF16)
    ssq = _dot(sq_hi, head_sum) + _dot(sq_lo, head_sum)
    ms = [ssq[(j % half) * tm:(j % half + 1) * tm, (j // half) * LANE:(j // half + 1) * LANE] * (1.0 / NSA_DH)
          for j in range(len(tiles))]
    normed = [t * lax.rsqrt(m + RMS_EPS) * g for t, m, g in zip(tiles, ms, gains)]
    hq_ref[...] = mm(C_HG, C_HG + W)
    roped = [t * c + pltpu.roll(t, LANE - ROPE_DIM // 2, 1) * sa + pltpu.roll(t, ROPE_DIM // 2, 1) * sb for t in normed]
    hv_ref[...] = mm(C_HG + 2 * W, C_HG + 3 * W).astype(BF16)

    def head_slots(tile):
        return jnp.where(low, tile, 0.0), jnp.where(low, pltpu.roll(tile, NSA_DH, 1), 0.0)

    pos = (pl.program_id(0) % tiles_per_seq) * tm + lax.broadcasted_iota(jnp.int32, (tm, LANE), 0)
    blk_onehot = jnp.where(lane - NSA_DH == jnp.right_shift(pos, SLC_SHIFT), 1.0, 0.0)
    for pr in range(nq):
        for k, (a, b) in enumerate(zip(head_slots(normed[pr]), head_slots(roped[pr]))):
            sl = slice((2 * pr + k) * LANE, (2 * pr + k + 1) * LANE)
            qn_ref[:, sl] = a.astype(BF16)
            qr_ref[:, sl] = b.astype(BF16)
    for kv, (ks, kw) in enumerate(zip(head_slots(roped[nq]), head_slots(roped[nq + 1]))):
        sl = slice(kv * LANE, (kv + 1) * LANE)
        ksl_ref[:, sl] = jnp.where(low, ks, blk_onehot).astype(BF16)
        kwn_ref[:, sl] = kw.astype(BF16)
    for ref, t in ((vsl_ref, kv_s[:, KV_W:]), (vwn_ref, kv_w[:, KV_W:])):
        for kv, v in enumerate((t, pltpu.roll(t, NSA_DH, 1))):
            ref[:, kv * V_SLOT:(kv + 1) * V_SLOT] = jnp.where(low, v, 1.0).astype(BF16)

    hgate_ref[...] = _sigmoid(mm(C_HG + 3 * W, C_HG + 4 * W)) * _silu(mm(C_HG + 4 * W, C_HG + 5 * W))


def _of_layer(a, layer, **kw):
    return pl.BlockSpec((None,) + a.shape[1:], lambda *_: (layer,) + (0,) * (a.ndim - 1), **kw)


def _proj_call(h2, g, w, w_tail, cos_t, sa_t, sb_t, qg, kg, lbv, layer, seq, tm):
    n = h2.shape[0]
    tpb = seq // tm
    row = lambda width: pl.BlockSpec((tm, width), lambda i: (i, 0))
    tab = pl.BlockSpec((tm, LANE), lambda i: (i % tpb, 0))
    k_slots, v_slots = NSA_KV_HEADS * LANE, NSA_KV_HEADS * V_SLOT
    out_w = [(HG_W, F32), (HG_W, F32), (HG_W, F32), (HG_W, BF16), (HG_W, F32),
             (NSA_HEADS * LANE, BF16), (NSA_HEADS * LANE, BF16), (KV_W, F32), (KV_W, F32),
             (k_slots, BF16), (v_slots, BF16), (k_slots, BF16), (v_slots, BF16), (LANE, F32), (NSA_W, F32)]
    return pl.pallas_call(
        functools.partial(_proj_kernel, tiles_per_seq=tpb),
        grid=(n // tm,),
        in_specs=[row(D_MODEL), _of_layer(g, layer), _of_layer(w, layer, pipeline_mode=pl.Buffered(1)),
                  _of_layer(w_tail, layer, pipeline_mode=pl.Buffered(1)), tab, tab, tab,
                  _of_layer(qg, layer), _of_layer(kg, layer), _of_layer(lbv, layer)],
        out_specs=[row(wd) for wd, _ in out_w],
        out_shape=[jax.ShapeDtypeStruct((n, wd), dt) for wd, dt in out_w],
        compiler_params=pltpu.CompilerParams(dimension_semantics=("parallel",), vmem_limit_bytes=VMEM_LIMIT),
        name="in_proj",
    )(h2, g, w, w_tail, cos_t, sa_t, sb_t, qg, kg, lbv)


def _hgrn_kernel(q_ref, b_ref, c_ref, v_ref, gate_ref, og_ref, y_ref, st_ref, u_ref, *, tb):
    C, SUB, HALF = HG_CHUNK, HG_SUB, HG_SUB // 2
    nch, nblk = tb // C, C // SUB

    @pl.when(pl.program_id(2) == 0)
    def _():
        st_ref[...] = jnp.zeros_like(st_ref)

    og = og_ref[...]
    lane8 = lax.broadcasted_iota(jnp.int32, (HALF, LANE), 1)
    lane_c = lax.broadcasted_iota(jnp.int32, (SUB, C), 1)
    row_c = lax.broadcasted_iota(jnp.int32, (SUB, C), 0)
    left_col = jnp.where(lax.broadcasted_iota(jnp.int32, (2 * HG_DK, 2 * LANE), 1) < LANE, 1.0, 0.0)
    sum_w = jnp.where(lax.broadcasted_iota(jnp.int32, (2 * HG_DK, 2 * LANE), 0) < HG_DK,
                      left_col, 1.0 - left_col).astype(BF16)

    q = [q_ref[k * C:(k + 1) * C, :] for k in range(nch)]
    b = [b_ref[k * C:(k + 1) * C, :] for k in range(nch)]
    c = [c_ref[k * C:(k + 1) * C, :] for k in range(nch)]
    v16 = [v_ref[k * C:(k + 1) * C, :] for k in range(nch)]

    left, right = [], []
    for k in range(nch):
        for blk in range(nblk):
            r0 = blk * SUB
            q_t, q_b = q[k][r0:r0 + HALF], q[k][r0 + HALF:r0 + SUB]
            b_t, b_b = b[k][r0:r0 + HALF], b[k][r0 + HALF:r0 + SUB]
            tops = [q_t * jnp.exp2(b_t - c[k][r0 + s:r0 + s + 1, :]) for s in range(HALF)]
            bots = [q_b * jnp.exp2(b_b - c[k][r0 + s:r0 + s + 1, :]) for s in range(SUB)]
            left += tops + bots[:HALF // 2]
            right += bots[HALF // 2:]
    per_blk = HALF + HALF // 2
    xs = jnp.concatenate([jnp.concatenate(left, axis=0), jnp.concatenate(right, axis=0)], axis=1)
    rs = _dot(xs.astype(BF16), sum_w)

    a = []
    for k in range(nch):
        a_rows = []
        for blk in range(nblk):
            r0 = blk * SUB
            base = (k * nblk + blk) * per_blk * HALF

            def piece(idx, side):
                return rs[base + idx * HALF:base + (idx + 1) * HALF, side * LANE:(side + 1) * LANE]

            a_t = jnp.zeros((HALF, LANE), F32)
            a_b = jnp.zeros((HALF, LANE), F32)
            for s in range(HALF):
                a_t = jnp.where(lane8 == s, piece(s, 0), a_t)
            for s in range(SUB):
                src = piece(HALF + s, 0) if s < HALF // 2 else piece(s - HALF // 2, 1)
                a_b = jnp.where(lane8 == s, src, a_b)
            a_d = jnp.concatenate([a_t, a_b], axis=0)
            if blk > 0:
                ref_b = b[k][r0 - 1:r0, :]
                q_p = (q[k][r0:r0 + SUB] * jnp.exp2(b[k][r0:r0 + SUB] - ref_b)).astype(BF16)
                k_p = jnp.concatenate([jnp.exp2(ref_b - c[k][:r0]), jnp.zeros((C - r0, HG_DK), F32)], axis=0)
                a_i = pltpu.roll(a_d, r0, 1)[:, :C] + _dot_nt(q_p, k_p.astype(BF16))
            else:
                a_i = a_d[:, :C]
            a_rows.append(jnp.where(lane_c <= r0 + row_c, a_i, 0.0))
        a.append(jnp.concatenate(a_rows, axis=0).astype(BF16))

    for k in range(nch):
        u_ref[k] = _dot_tn(v16[k], jnp.exp2(b[k][C - 1:C, :] - c[k]).astype(BF16))

    st = st_ref[...]
    for k in range(nch):
        o = _dot(a[k], v16[k]) + _dot_nt((q[k] * jnp.exp2(b[k])).astype(BF16), st.astype(BF16))
        st = st * jnp.exp2(b[k][C - 1:C, :]) + u_ref[k]
        ms = jnp.mean(o * o, axis=-1, keepdims=True)
        y_ref[k * C:(k + 1) * C, :] = o * lax.rsqrt(ms + RMS_EPS) * og * gate_ref[k * C:(k + 1) * C, :]
    st_ref[...] = st


def _hgrn_call(hq, lf, lk, hv, hgate, og, layer, tb):
    bsz, seq, _ = hq.shape
    piece = pl.BlockSpec((None, tb, HG_DK), lambda b, h, i: (b, i, h))
    return pl.pallas_call(
        functools.partial(_hgrn_kernel, tb=tb),
        grid=(bsz, HG_HEADS, seq // tb),
        in_specs=[piece, piece, piece, piece, piece, _of_layer(og, layer)],
        out_specs=piece,
        out_shape=jax.ShapeDtypeStruct((bsz, seq, HG_W), F32),
        scratch_shapes=[pltpu.VMEM((HG_DV, HG_DK), F32), pltpu.VMEM((tb // HG_CHUNK, HG_DV, HG_DK), F32)],
        compiler_params=pltpu.CompilerParams(dimension_semantics=("parallel", "parallel", "arbitrary"),
                                             vmem_limit_bytes=VMEM_LIMIT),
        name="hgrn2",
    )(hq, lf, lk, hv, hgate, og)


def _cmp_kernel(kcm_ref, vcm_ref, pe_ref, w1_ref, w2_ref, g_ref, kc_ref, vc_ref):
    half = CMP_BLOCK // 2
    ncp = kcm_ref.shape[0] // CMP_STRIDE
    for which, (src, out_ref) in enumerate(((kcm_ref, kc_ref), (vcm_ref, vc_ref))):
        first = jnp.zeros((ncp, NSA_KV_HEADS * CMP_HIDDEN), F32)
        second = jnp.zeros((ncp, NSA_KV_HEADS * CMP_HIDDEN), F32)
        for r in range(half):
            tok = src[pl.ds(r, ncp, stride=CMP_STRIDE), :]
            first = first + _dot((tok + pe_ref[which, r:r + 1, :]).astype(BF16), w1_ref[which, r])
            second = second + _dot((tok + pe_ref[which, half + r:half + r + 1, :]).astype(BF16), w1_ref[which, half + r])
        hid = first + pltpu.roll(second, ncp - 1, 0)
        out = _dot(_silu(hid).astype(BF16), w2_ref[which])
        if which == 0:
            for kv in range(NSA_KV_HEADS):
                sl = slice(kv * LANE, (kv + 1) * LANE)
                out_ref[:, sl] = _slot_norm(out[:, sl], g_ref[...]).astype(BF16)
        else:
            head_lane = jnp.bitwise_and(lax.broadcasted_iota(jnp.int32, out.shape, 1), LANE - 1)
            out_ref[...] = jnp.where(head_lane < NSA_DH, out, 1.0).astype(BF16)


def _cmp_call(kcm, vcm, pe, w1, w2, g, layer):
    bsz, seq, _ = kcm.shape
    ncp = seq // CMP_STRIDE
    slots = NSA_KV_HEADS * LANE
    tok = pl.BlockSpec((None, seq, KV_W), lambda b: (b, 0, 0))
    out = pl.BlockSpec((None, ncp, slots), lambda b: (b, 0, 0))
    return pl.pallas_call(
        _cmp_kernel,
        grid=(bsz,),
        in_specs=[tok, tok, _of_layer(pe, layer), _of_layer(w1, layer), _of_layer(w2, layer), _of_layer(g, layer)],
        out_specs=[out, out],
        out_shape=[jax.ShapeDtypeStruct((bsz, ncp, slots), BF16)] * 2,
        compiler_params=pltpu.CompilerParams(dimension_semantics=("parallel",), vmem_limit_bytes=VMEM_LIMIT),
        name="compress",
    )(kcm, vcm, pe, w1, w2, g)


def _attend(q, k, v, bias, acc_ref, m_ref):
    p, m_new = _exp_tiles(_score_tiles(q, k, bias), m_ref[...])
    acc_ref[...] = jnp.exp2(m_ref[...] - m_new) * acc_ref[...] + _pv_tiles(p, v)
    m_ref[...] = m_new


def _score_tiles(q, k, bias):
    pieces = []
    for c in range(k.shape[0] // MXU_COLS):
        s = _dot_nt(q, k[c * MXU_COLS:(c + 1) * MXU_COLS])
        pieces.append(s if bias is None else s + bias[:, c * MXU_COLS:(c + 1) * MXU_COLS])
    return pieces


def _exp_tiles(pieces, m_floor):
    row_max = functools.reduce(jnp.maximum, [jnp.max(s, axis=-1, keepdims=True) for s in pieces])
    m = row_max if m_floor is None else jnp.maximum(m_floor, row_max)
    m_wide = jnp.concatenate([jnp.broadcast_to(m, (pieces[0].shape[0], LANE))] * (MXU_COLS // LANE), axis=1)
    return [jnp.exp2(s - m_wide).astype(BF16) for s in pieces], m


def _pv_tiles(p, v):
    return functools.reduce(lambda a, b: a + b,
                            [_dot(pc, v[c * MXU_COLS:(c + 1) * MXU_COLS]) for c, pc in enumerate(p)])


def _nsa_kernel(qn_ref, qr_ref, kc_ref, vc_ref, ksl_ref, vsl_ref, kwn_ref, vwn_ref, ng_ref, nz_ref,
                ovt_ref, gexp_ref, y_ref, qa_ref, ocmp_ref, acc_s_ref, m_s_ref, acc_w_ref, *, tq, tk):
    G = NSA_G
    R = G * tq
    i = pl.program_id(2)
    t0 = i * tq
    ncp = kc_ref.shape[0]
    nsb = ovt_ref.shape[0] - 8
    topk = min(SLC_TOPK, nsb)
    lane = lax.broadcasted_iota(jnp.int32, (tq, LANE), 1)

    def heads_on_rows(ref):
        return jnp.concatenate([ref[:, g * LANE:(g + 1) * LANE] for g in range(G)], axis=0)

    def all_heads(a):
        return jnp.concatenate([a] * G, axis=0)

    q_pos = t0 + lax.broadcasted_iota(jnp.int32, (tq, 1), 0)

    def position_bias(start, width, window=None):
        dist = q_pos - (start + lax.broadcasted_iota(jnp.int32, (tq, width), 1))
        ok = jnp.where(dist >= 0, 0.0, NEG)
        return all_heads(ok if window is None else jnp.where(dist < window, ok, NEG))

    def _compress_and_select(qn_src, qr_src, base):
        pos_col = base + lax.broadcasted_iota(jnp.int32, (tq, 1), 0)
        c_end = lax.broadcasted_iota(jnp.int32, (tq, ncp), 1) * CMP_STRIDE + (CMP_BLOCK - 1)
        s = _dot_nt(heads_on_rows(qn_src), kc_ref[...]) + all_heads(jnp.where(c_end <= pos_col, 0.0, NEG))
        e = jnp.exp2(s - jnp.max(s, axis=-1, keepdims=True))
        e_hi = e.astype(BF16)
        e_lo = (e - e_hi.astype(F32)).astype(BF16)
        ocmp_ref[...] = _dot(e_hi, vc_ref[...])

        ovt = ovt_ref[...]
        raw_t = _dot_nt(ovt, e_hi) + _dot_nt(ovt, e_lo)
        any_valid_t = jnp.where(base + lax.broadcasted_iota(jnp.int32, (1, tq), 1) >= CMP_BLOCK - 1, 1.0, 0.0)
        imp_t = jnp.zeros((nsb, tq), F32)
        for g in range(G):
            imp_t = imp_t + raw_t[:nsb, g * tq:(g + 1) * tq] * (any_valid_t / raw_t[nsb:nsb + 1, g * tq:(g + 1) * tq])
        j_row = lax.broadcasted_iota(jnp.int32, (nsb, tq), 0)
        cur = jnp.right_shift(base + lax.broadcasted_iota(jnp.int32, (nsb, tq), 1), SLC_SHIFT)
        forced = (j_row == 0) | (j_row == cur) | (j_row == cur - 1)
        score = jnp.where(forced, FORCE_SCORE, jnp.where(j_row <= cur, imp_t, -1.0))
        yield
        ngrp = nsb // 8
        sub8 = lax.broadcasted_iota(jnp.int32, (8, LANE), 0)
        pen_cols, steps = [], 0
        for qt in range(tq // LANE):
            sc = score[:, qt * LANE:(qt + 1) * LANE]
            grp = [sc[8 * a:8 * a + 8] for a in range(ngrp)]
            beats = [jnp.zeros((8, LANE), F32) for _ in range(ngrp)]
            for jp in range(nsb):
                row = sc[jp:jp + 1, :]
                for a in range(ngrp):
                    if a > jp // 8:
                        hit = jnp.where(row >= grp[a], 1.0, 0.0)
                    elif a < jp // 8:
                        hit = jnp.where(row > grp[a], 1.0, 0.0)
                    else:
                        hit = jnp.where(sub8 > jp % 8, jnp.where(row >= grp[a], 1.0, 0.0),
                                        jnp.where(row > grp[a], 1.0, 0.0))
                    beats[a] = beats[a] + hit
                steps += 1
                if steps % (nsb * (tq // LANE) // RANK_STAGES) == 0 and steps < nsb * (tq // LANE):
                    yield
            pen_cols.append(jnp.concatenate([jnp.where(b < topk, 0.0, PENALTY) for b in beats], axis=0))
        pen_t = jnp.concatenate(pen_cols, axis=1)
        zeros_t = jnp.zeros((NSA_DH, tq), F32)
        pad_t = [zeros_t[:LANE - NSA_DH - nsb]] if nsb < LANE - NSA_DH else []
        pen = jnp.concatenate([zeros_t, pen_t] + pad_t, axis=0).T.astype(BF16)
        for g in range(G):
            qa_ref[g * tq:(g + 1) * tq, :] = jnp.where(lane < NSA_DH, qr_src[:, g * LANE:(g + 1) * LANE], pen)

    select = _compress_and_select(qn_ref, qr_ref, t0)

    q4 = heads_on_rows(qr_ref)
    w0 = pl.multiple_of(jnp.maximum(t0 - 2 * tq, 0), tq)
    s_w = _score_tiles(q4, kwn_ref[pl.ds(w0, 3 * tq), :], position_bias(w0, 3 * tq, WINDOW))
    next(select)
    p_w, _ = _exp_tiles(s_w, None)
    next(select)
    acc_w_ref[...] = _pv_tiles(p_w, vwn_ref[pl.ds(w0, 3 * tq), :])
    for _ in select:
        pass

    qa = qa_ref[...]
    top = pl.multiple_of((i // 2) * tk, tk)
    p_top, m_top = _exp_tiles(_score_tiles(qa, ksl_ref[pl.ds(top, tk), :], position_bias(top, tk)), None)
    acc_s_ref[...] = _pv_tiles(p_top, vsl_ref[pl.ds(top, tk), :])
    m_s_ref[...] = jnp.broadcast_to(m_top, (R, LANE))

    def tiles_below(first, count):
        for c in range(count):
            r = pl.multiple_of((first + c) * tk, tk)
            _attend(qa, ksl_ref[pl.ds(r, tk), :], vsl_ref[pl.ds(r, tk), :], None, acc_s_ref, m_s_ref)

    n_below = i // 2
    lax.fori_loop(0, n_below // 4, lambda j, carry: tiles_below(4 * j, 4) or carry, 0)
    lax.fori_loop(0, (n_below % 4) // 2, lambda j, carry: tiles_below(n_below // 4 * 4, 2) or carry, 0)
    lax.fori_loop(0, n_below % 2, lambda j, carry: tiles_below(n_below - 1, 1) or carry, 0)

    sig = _sigmoid(ng_ref[...])
    sig_hi = sig.astype(BF16)
    sig_lo = (sig - sig_hi.astype(F32)).astype(BF16)
    gates = _dot(sig_hi, gexp_ref[...]) + _dot(sig_lo, gexp_ref[...])
    low = lane < NSA_DH
    any_valid = jnp.where(t0 + lax.broadcasted_iota(jnp.int32, (tq, LANE), 0) >= CMP_BLOCK - 1, 1.0, 0.0)
    o_cmp = ocmp_ref
    for pair in range(G // 2):
        ev = slice(2 * pair * tq, (2 * pair + 1) * tq)
        od = slice((2 * pair + 1) * tq, (2 * pair + 2) * tq)

        def gate(branch):
            lo = (branch * (G // 2) + pair) * LANE
            return gates[:, lo:lo + LANE]

        def normalised(acc):
            even, odd = acc[ev, :], acc[od, :]
            even_sw, odd_sw = pltpu.roll(even, NSA_DH, 1), pltpu.roll(odd, NSA_DH, 1)
            return jnp.where(low, even, odd_sw) / jnp.where(low, even_sw, odd)

        o = (gate(0) * any_valid * normalised(o_cmp) + gate(1) * normalised(acc_s_ref)
             + gate(2) * normalised(acc_w_ref))
        y_ref[:, pair * LANE:(pair + 1) * LANE] = o * _silu(nz_ref[:, pair * LANE:(pair + 1) * LANE])


def _nsa_call(qn, qr, kc, vc, ksl, vsl, kwn, vwn, ng, nz, ovt, gexp, tq, tk):
    bsz, seq, _ = qn.shape
    ncp = kc.shape[1]
    rows = NSA_G * tq
    per_kv = lambda nrows, width: pl.BlockSpec((None, nrows, width), lambda b, k, i: (b, 0, k))
    qspec = pl.BlockSpec((None, tq, NSA_G * LANE), lambda b, k, i: (b, i, k))
    yspec = pl.BlockSpec((None, tq, NSA_G * NSA_DH), lambda b, k, i: (b, i, k))
    return pl.pallas_call(
        functools.partial(_nsa_kernel, tq=tq, tk=tk),
        grid=(bsz, NSA_KV_HEADS, seq // tq),
        in_specs=[qspec, qspec, per_kv(ncp, LANE), per_kv(ncp, V_SLOT), per_kv(seq, LANE),
                  per_kv(seq, V_SLOT), per_kv(seq, LANE), per_kv(seq, V_SLOT),
                  pl.BlockSpec((None, tq, LANE), lambda b, k, i: (b, i, 0)), yspec,
                  pl.BlockSpec(ovt.shape, lambda b, k, i: (0, 0)),
                  pl.BlockSpec((None,) + gexp.shape[1:], lambda b, k, i: (k, 0, 0))],
        out_specs=yspec,
        out_shape=jax.ShapeDtypeStruct((bsz, seq, NSA_W), F32),
        scratch_shapes=[pltpu.VMEM((rows, LANE), BF16), pltpu.VMEM((rows, LANE), F32),
                        pltpu.VMEM((rows, V_SLOT), F32), pltpu.VMEM((rows, LANE), F32), pltpu.VMEM((rows, V_SLOT), F32)],
        compiler_params=pltpu.CompilerParams(dimension_semantics=("parallel", "parallel", "parallel"),
                                             vmem_limit_bytes=VMEM_LIMIT),
        name="nsa_attn",
    )(qn, qr, kc, vc, ksl, vsl, kwn, vwn, ng, nz, ovt, gexp)


def _out_kernel(h_ref, yh_ref, yn_ref, p_ref, wo_ref, g_ref, wpg_ref, wpp_ref, o_ref):
    h = h_ref[...] + _dot(yh_ref[...].astype(BF16), wo_ref[0:HG_W, :]) + _dot(yn_ref[...].astype(BF16), wo_ref[HG_W:, :])
    var = jnp.mean(h * h, axis=-1, keepdims=True)
    hn = (h * lax.rsqrt(var + RMS_EPS) * g_ref[...]).astype(BF16)
    gate = _sigmoid(_dot(hn, wpg_ref[...]))
    o_ref[...] = h + gate * _dot(p_ref[...].astype(BF16), wpp_ref[...])


def _out_call(h2, yh, yn, p3, wo, g, wpg, wpp, layer, tm):
    n = h2.shape[0]
    row = lambda width: pl.BlockSpec((tm, width), lambda i: (i, 0))
    return pl.pallas_call(
        _out_kernel,
        grid=(n // tm,),
        in_specs=[row(D_MODEL), row(HG_W), row(NSA_W), pl.BlockSpec((None, tm, PLE_DIM), lambda i: (layer, i, 0)),
                  _of_layer(wo, layer), _of_layer(g, layer), _of_layer(wpg, layer), _of_layer(wpp, layer)],
        out_specs=row(D_MODEL),
        out_shape=jax.ShapeDtypeStruct((n, D_MODEL), F32),
        compiler_params=pltpu.CompilerParams(dimension_semantics=("parallel",), vmem_limit_bytes=VMEM_LIMIT),
        name="out_proj_ple",
    )(h2, yh, yn, p3, wo, g, wpg, wpp)


def _pad_last(a, width):
    return jnp.pad(a, [(0, 0)] * (a.ndim - 1) + [(0, width - a.shape[-1])])


def _layout_w_in(w_in):
    n_gate = 3 * NSA_HEADS
    ngate, nz = w_in[..., C_NG:C_NG + n_gate], w_in[..., C_NG + n_gate:]
    return w_in.astype(BF16), jnp.concatenate([_pad_last(ngate, LANE), nz], axis=-1).astype(BF16)


def _layout_cmp(pe, w1, w2):
    kvh = NSA_KV_HEADS
    lead = w1.shape[:2]

    def block_diag(a):
        rows = [jnp.concatenate([a if g == k else jnp.zeros_like(a) for k in range(kvh)], axis=-1) for g in range(kvh)]
        return jnp.concatenate(rows, axis=-2)

    w1x = block_diag(w1.astype(BF16).reshape(lead + (CMP_BLOCK, NSA_DH, CMP_HIDDEN)))
    w2x = block_diag(_pad_last(w2, LANE).astype(BF16))
    pe2 = jnp.tile(pe, (1, 1, 1, kvh))
    return pe2, w1x, w2x


def kernel(x, p, norm_g, w_in, hgrn_lb, hgrn_onorm_g, nsa_qnorm_g, nsa_knorm_g, cmp_pe, cmp_w1, cmp_w2, w_out,
           ple_norm_g, w_pg, w_pp):
    bsz, seq, _ = x.shape
    depth = w_in.shape[0]
    n = bsz * seq
    tm = 256
    tq, tk, tb = WINDOW // 2, WINDOW, 2048
    assert seq % tk == 0 and seq // SLC_BLOCK <= LANE - NSA_DH and (seq // SLC_BLOCK) % 8 == 0
    ncp = seq // CMP_STRIDE
    nsb = seq // SLC_BLOCK

    pos = jnp.arange(seq, dtype=F32)
    inv = ROPE_THETA ** (-jnp.arange(0, ROPE_DIM, 2, dtype=F32) / ROPE_DIM)
    ang = pos[:, None] * inv[None, :]
    cos, sin = jnp.cos(ang), jnp.sin(ang)
    half = ROPE_DIM // 2
    per_head = lambda t: jnp.tile(_pad_last(t, NSA_DH), (1, LANE // NSA_DH))
    cos_t = per_head(jnp.concatenate([cos, cos, jnp.ones((seq, NSA_DH - ROPE_DIM), F32)], axis=1))
    sa_t = per_head(-sin)
    sb_t = per_head(jnp.concatenate([jnp.zeros((seq, half), F32), sin], axis=1))

    lb_all = jnp.cumsum(jax.nn.softmax(hgrn_lb.astype(F32), axis=0), axis=0)
    lb_all = lb_all - lb_all[0]
    lbv = jnp.stack([jnp.log(lb_all), jnp.log1p(-lb_all)], axis=1)

    c_tok = jnp.arange(ncp)[:, None] * CMP_STRIDE + jnp.arange(CMP_BLOCK)[None, :]
    overlap = jnp.mean((c_tok[..., None] // SLC_BLOCK == jnp.arange(nsb)).astype(F32), axis=1)
    ovt = jnp.concatenate([overlap.T, jnp.ones((8, ncp), F32)], axis=0).astype(BF16)
    col = jnp.arange(3 * (NSA_G // 2) * LANE)
    src = (col // (NSA_G // 2 * LANE)) * NSA_HEADS + ((col // LANE) % (NSA_G // 2)) * 2 + (col % LANE) // NSA_DH
    gexp = (jnp.arange(LANE)[None, :, None] == src[None, None, :] + NSA_G * jnp.arange(NSA_KV_HEADS)[:, None, None]).astype(BF16)

    w_all, w_tail = _layout_w_in(w_in)
    wo16, wpg16, wpp16 = w_out.astype(BF16), w_pg.astype(BF16), w_pp.astype(BF16)
    row_vec = lambda a: a[:, None, :]
    qg_all = row_vec(jnp.tile(nsa_qnorm_g, (1, LANE // NSA_DH)))
    kg_tiled = jnp.tile(nsa_knorm_g[:, 1:], (1, 1, LANE // NSA_DH))
    kg_cmp = _pad_last(nsa_knorm_g[:, 0:1], LANE)
    cmp_params = _layout_cmp(cmp_pe, cmp_w1, cmp_w2)
    p3 = p.reshape(depth, n, PLE_DIM)

    h = x.reshape(n, D_MODEL)
    for i in range(depth):
        r3 = lambda a: a.reshape(bsz, seq, -1)
        (hq, lf, lk, hv, hgate, qn, qr, kcm, vcm, ksl, vsl, kwn, vwn, ng, nz) = _proj_call(
            h, row_vec(norm_g), w_all, w_tail, cos_t, sa_t, sb_t, qg_all, kg_tiled, lbv, i, seq, 2 * tm)
        y_hg = _hgrn_call(r3(hq), r3(lf), r3(lk), r3(hv), r3(hgate), row_vec(hgrn_onorm_g), i, tb)
        kc, vc = _cmp_call(r3(kcm), r3(vcm), *cmp_params, kg_cmp, i)
        y_nsa = _nsa_call(r3(qn), r3(qr), kc, vc, r3(ksl), r3(vsl), r3(kwn), r3(vwn), r3(ng), r3(nz), ovt, gexp,
                          tq, tk)
        h = _out_call(h, y_hg.reshape(n, HG_W), y_nsa.reshape(n, NSA_W), p3, wo16, row_vec(ple_norm_g), wpg16, wpp16,
                      i, 2 * tm)
    return h.reshape(bsz, seq, D_MODEL)
```

```python
import functools

import jax
import jax.numpy as jnp
from jax import lax
from jax.experimental import pallas as pl
from jax.experimental.pallas import tpu as pltpu

F32 = jnp.float32
BF16 = jnp.bfloat16

D_MODEL = 1024
HG_HEADS = 4
HG_DK = 128
HG_DV = 128
HG_W = HG_HEADS * HG_DV
HG_CHUNK = 64
HG_SUB = 16
NSA_HEADS = 8
NSA_KV_HEADS = 2
NSA_DH = 64
NSA_G = NSA_HEADS // NSA_KV_HEADS
NSA_W = NSA_HEADS * NSA_DH
KV_W = NSA_KV_HEADS * NSA_DH
CMP_BLOCK = 32
CMP_STRIDE = 16
CMP_HIDDEN = 128
SLC_BLOCK = 64
SLC_TOPK = 16
WINDOW = 512
FORCE_SCORE = 1e4
ROPE_THETA = 500000.0
ROPE_DIM = NSA_DH // 4
PLE_DIM = 256
RMS_EPS = 1e-6
IN_SPLITS = (HG_W,) * 5 + (NSA_W,) + (KV_W,) * 6 + (3 * NSA_HEADS, NSA_W)

LANE = 128
SUBLANES = 8
NEG = -1e30
LOG2E = 1.4426950408889634
SLC_SHIFT = SLC_BLOCK.bit_length() - 1
MXU_COLS = 256
PENALTY = -(2.0 ** 100)
RANK_STAGES = 2
VMEM_LIMIT = 56 * 1024 * 1024

V_SLOT = LANE

C_HG = 0
C_Q = C_HG + 5 * HG_W
C_KCM = C_Q + NSA_W
C_KSL = C_KCM + 2 * KV_W
C_KWN = C_KSL + 2 * KV_W
C_NG = C_KWN + 2 * KV_W
C_NZ = C_NG + LANE
C_END = C_NZ + NSA_W


def _dot(a, b):
    return jnp.dot(a, b, preferred_element_type=F32)


def _dot_nt(a, b):
    return lax.dot_general(a, b, (((1,), (1,)), ((), ())), preferred_element_type=F32)


def _dot_tn(a, b):
    return lax.dot_general(a, b, (((0,), (0,)), ((), ())), preferred_element_type=F32)


def _sigmoid(x):
    return 1.0 / (1.0 + jnp.exp(-x))


def _silu(x):
    return x * _sigmoid(x)


def _slot_norm(x, g):
    ms = jnp.sum(x * x, axis=-1, keepdims=True) * (1.0 / NSA_DH)
    return x * lax.rsqrt(ms + RMS_EPS) * g


def _proj_kernel(h_ref, g_ref, w_ref, wtail_ref, cos_ref, sa_ref, sb_ref, qg_ref, kg_ref, lb_ref,
                 hq_ref, lf_ref, lk_ref, hv_ref, hgate_ref, qn_ref, qr_ref, kcm_ref, vcm_ref, ksl_ref, vsl_ref,
                 kwn_ref, vwn_ref, ng_ref, nz_ref, *, tiles_per_seq):
    x = h_ref[...]
    var = jnp.mean(x * x, axis=-1, keepdims=True)
    xn = (x * lax.rsqrt(var + RMS_EPS) * g_ref[...]).astype(BF16)

    def mm(a, b):
        return _dot(xn, w_ref[:, a:b])

    W = HG_W
    tm = x.shape[0]

    def decay_prep(fl, head):
        nrows, row0 = tm, 0
        cols = slice(head * LANE, (head + 1) * LANE)
        loglb, l1mlb = lb_ref[0:1, cols], lb_ref[1:2, cols]
        l1pe = jnp.log(1.0 + jnp.exp(-jnp.abs(fl)))
        cc = l1mlb + (jnp.minimum(fl, 0.0) - l1pe)
        lf = (jnp.maximum(loglb, cc) + jnp.log(1.0 + jnp.exp(-jnp.abs(loglb - cc)))) * LOG2E
        lk = (l1mlb - jnp.maximum(fl, 0.0) - l1pe) * LOG2E
        t3 = lf.reshape(nrows // SUBLANES, SUBLANES, LANE)
        sub = lax.broadcasted_iota(jnp.int32, t3.shape, 1)
        for step in (1, 2, 4):
            t3 = t3 + jnp.where(sub >= step, pltpu.roll(t3, step, 1), 0.0)
        per_chunk = HG_CHUNK // SUBLANES
        for tile in range(nrows // SUBLANES):
            rows = slice(tile * SUBLANES, (tile + 1) * SUBLANES)
            out_rows = slice(row0 + tile * SUBLANES, row0 + (tile + 1) * SUBLANES)
            total = t3[tile, SUBLANES - 1:SUBLANES, :]
            if tile % per_chunk == 0:
                b, run = t3[tile], total
            else:
                b, run = t3[tile] + run, run + total
            lf_ref[out_rows, cols] = b
            lk_ref[out_rows, cols] = b - lk[rows]

    fl_a = mm(C_HG + W, C_HG + W + W // 2)
    qs = mm(C_Q, C_KCM)
    decay_prep(fl_a[:, :LANE], 0)
    kvc = mm(C_KCM, C_KSL)
    kv_s = mm(C_KSL, C_KWN)
    decay_prep(fl_a[:, LANE:], 1)
    fl_b = mm(C_HG + W + W // 2, C_HG + 2 * W)
    kv_w = mm(C_KWN, C_NG)
    gz = _dot(xn, wtail_ref[...])
    decay_prep(fl_b[:, :LANE], 2)
    decay_prep(fl_b[:, LANE:], 3)

    c, sa, sb = cos_ref[...], sa_ref[...], sb_ref[...]
    scale = NSA_DH ** -0.5 * LOG2E
    lane = lax.broadcasted_iota(jnp.int32, (tm, LANE), 1)
    low = lane < NSA_DH
    kcm_ref[...] = kvc[:, :KV_W]
    vcm_ref[...] = kvc[:, KV_W:]
    ng_ref[...] = gz[:, :LANE]
    nz_ref[...] = gz[:, LANE:]

    nq = NSA_HEADS // 2
    tiles = [qs[:, pr * LANE:(pr + 1) * LANE] for pr in range(nq)] + [kv_s[:, :KV_W], kv_w[:, :KV_W]]
    gains = [qg_ref[...] * scale] * nq + [kg_ref[0:1, :], kg_ref[1:2, :]]
    half = len(tiles) // 2
    sq = jnp.concatenate([jnp.concatenate([t * t for t in tiles[:half]], axis=0),
                          jnp.concatenate([t * t for t in tiles[half:]], axis=0)], axis=1)
    same_head = (jnp.right_shift(lax.broadcasted_iota(jnp.int32, (2 * LANE, 2 * LANE), 0), NSA_DH.bit_length() - 1)
                 == jnp.right_shift(lax.broadcasted_iota(jnp.int32, (2 * LANE, 2 * LANE), 1), NSA_DH.bit_length() - 1))
    head_sum = jnp.where(same_head, 1.0, 0.0).astype(BF16)
    sq_hi = sq.astype(BF16)
    sq_lo = (sq - sq_hi.astype(F32)).astype(BF16)
    ssq = _dot(sq_hi, head_sum) + _dot(sq_lo, head_sum)
    ms = [ssq[(j % half) * tm:(j % half + 1) * tm, (j // half) * LANE:(j // half + 1) * LANE] * (1.0 / NSA_DH)
          for j in range(len(tiles))]
    normed = [t * lax.rsqrt(m + RMS_EPS) * g for t, m, g in zip(tiles, ms, gains)]
    hq_ref[...] = mm(C_HG, C_HG + W)
    roped = [t * c + pltpu.roll(t, LANE - ROPE_DIM // 2, 1) * sa + pltpu.roll(t, ROPE_DIM // 2, 1) * sb for t in normed]
    hv_ref[...] = mm(C_HG + 2 * W, C_HG + 3 * W).astype(BF16)

    def head_slots(tile):
        return jnp.where(low, tile, 0.0), jnp.where(low, pltpu.roll(tile, NSA_DH, 1), 0.0)

    pos = (pl.program_id(0) % tiles_per_seq) * tm + lax.broadcasted_iota(jnp.int32, (tm, LANE), 0)
    blk_onehot = jnp.where(lane - NSA_DH == jnp.right_shift(pos, SLC_SHIFT), 1.0, 0.0)
    for pr in range(nq):
        for k, (a, b) in enumerate(zip(head_slots(normed[pr]), head_slots(roped[pr]))):
            sl = slice((2 * pr + k) * LANE, (2 * pr + k + 1) * LANE)
            qn_ref[:, sl] = a.astype(BF16)
            qr_ref[:, sl] = b.astype(BF16)
    for kv, (ks, kw) in enumerate(zip(head_slots(roped[nq]), head_slots(roped[nq + 1]))):
        sl = slice(kv * LANE, (kv + 1) * LANE)
        ksl_ref[:, sl] = jnp.where(low, ks, blk_onehot).astype(BF16)
        kwn_ref[:, sl] = kw.astype(BF16)
    for ref, t in ((vsl_ref, kv_s[:, KV_W:]), (vwn_ref, kv_w[:, KV_W:])):
        for kv, v in enumerate((t, pltpu.roll(t, NSA_DH, 1))):
            ref[:, kv * V_SLOT:(kv + 1) * V_SLOT] = jnp.where(low, v, 1.0).astype(BF16)

    hgate_ref[...] = _sigmoid(mm(C_HG + 3 * W, C_HG + 4 * W)) * _silu(mm(C_HG + 4 * W, C_HG + 5 * W))


def _of_layer(a, layer, **kw):
    return pl.BlockSpec((None,) + a.shape[1:], lambda *_: (layer,) + (0,) * (a.ndim - 1), **kw)


def _proj_call(h2, g, w, w_tail, cos_t, sa_t, sb_t, qg, kg, lbv, layer, seq, tm):
    n = h2.shape[0]
    tpb = seq // tm
    row = lambda width: pl.BlockSpec((tm, width), lambda i: (i, 0))
    tab = pl.BlockSpec((tm, LANE), lambda i: (i % tpb, 0))
    k_slots, v_slots = NSA_KV_HEADS * LANE, NSA_KV_HEADS * V_SLOT
    out_w = [(HG_W, F32), (HG_W, F32), (HG_W, F32), (HG_W, BF16), (HG_W, F32),
             (NSA_HEADS * LANE, BF16), (NSA_HEADS * LANE, BF16), (KV_W, F32), (KV_W, F32),
             (k_slots, BF16), (v_slots, BF16), (k_slots, BF16), (v_slots, BF16), (LANE, F32), (NSA_W, F32)]
    return pl.pallas_call(
        functools.partial(_proj_kernel, tiles_per_seq=tpb),
        grid=(n // tm,),
        in_specs=[row(D_MODEL), _of_layer(g, layer), _of_layer(w, layer, pipeline_mode=pl.Buffered(1)),
                  _of_layer(w_tail, layer, pipeline_mode=pl.Buffered(1)), tab, tab, tab,
                  _of_layer(qg, layer), _of_layer(kg, layer), _of_layer(lbv, layer)],
        out_specs=[row(wd) for wd, _ in out_w],
        out_shape=[jax.ShapeDtypeStruct((n, wd), dt) for wd, dt in out_w],
        compiler_params=pltpu.CompilerParams(dimension_semantics=("parallel",), vmem_limit_bytes=VMEM_LIMIT),
        name="in_proj",
    )(h2, g, w, w_tail, cos_t, sa_t, sb_t, qg, kg, lbv)


def _hgrn_kernel(q_ref, b_ref, c_ref, v_ref, gate_ref, og_ref, y_ref, st_ref, u_ref, *, tb):
    C, SUB, HALF = HG_CHUNK, HG_SUB, HG_SUB // 2
    nch, nblk = tb // C, C // SUB

    @pl.when(pl.program_id(2) == 0)
    def _():
        st_ref[...] = jnp.zeros_like(st_ref)

    og = og_ref[...]
    lane8 = lax.broadcasted_iota(jnp.int32, (HALF, LANE), 1)
    lane_c = lax.broadcasted_iota(jnp.int32, (SUB, C), 1)
    row_c = lax.broadcasted_iota(jnp.int32, (SUB, C), 0)
    left_col = jnp.where(lax.broadcasted_iota(jnp.int32, (2 * HG_DK, 2 * LANE), 1) < LANE, 1.0, 0.0)
    sum_w = jnp.where(lax.broadcasted_iota(jnp.int32, (2 * HG_DK, 2 * LANE), 0) < HG_DK,
                      left_col, 1.0 - left_col).astype(BF16)

    q = [q_ref[k * C:(k + 1) * C, :] for k in range(nch)]
    b = [b_ref[k * C:(k + 1) * C, :] for k in range(nch)]
    c = [c_ref[k * C:(k + 1) * C, :] for k in range(nch)]
    v16 = [v_ref[k * C:(k + 1) * C, :] for k in range(nch)]

    left, right = [], []
    for k in range(nch):
        for blk in range(nblk):
            r0 = blk * SUB
            q_t, q_b = q[k][r0:r0 + HALF], q[k][r0 + HALF:r0 + SUB]
            b_t, b_b = b[k][r0:r0 + HALF], b[k][r0 + HALF:r0 + SUB]
            tops = [q_t * jnp.exp2(b_t - c[k][r0 + s:r0 + s + 1, :]) for s in range(HALF)]
            bots = [q_b * jnp.exp2(b_b - c[k][r0 + s:r0 + s + 1, :]) for s in range(SUB)]
            left += tops + bots[:HALF // 2]
            right += bots[HALF // 2:]
    per_blk = HALF + HALF // 2
    xs = jnp.concatenate([jnp.concatenate(left, axis=0), jnp.concatenate(right, axis=0)], axis=1)
    rs = _dot(xs.astype(BF16), sum_w)

    a = []
    for k in range(nch):
        a_rows = []
        for blk in range(nblk):
            r0 = blk * SUB
            base = (k * nblk + blk) * per_blk * HALF

            def piece(idx, side):
                return rs[base + idx * HALF:base + (idx + 1) * HALF, side * LANE:(side + 1) * LANE]

            a_t = jnp.zeros((HALF, LANE), F32)
            a_b = jnp.zeros((HALF, LANE), F32)
            for s in range(HALF):
                a_t = jnp.where(lane8 == s, piece(s, 0), a_t)
            for s in range(SUB):
                src = piece(HALF + s, 0) if s < HALF // 2 else piece(s - HALF // 2, 1)
                a_b = jnp.where(lane8 == s, src, a_b)
            a_d = jnp.concatenate([a_t, a_b], axis=0)
            if blk > 0:
                ref_b = b[k][r0 - 1:r0, :]
                q_p = (q[k][r0:r0 + SUB] * jnp.exp2(b[k][r0:r0 + SUB] - ref_b)).astype(BF16)
                k_p = jnp.concatenate([jnp.exp2(ref_b - c[k][:r0]), jnp.zeros((C - r0, HG_DK), F32)], axis=0)
                a_i = pltpu.roll(a_d, r0, 1)[:, :C] + _dot_nt(q_p, k_p.astype(BF16))
            else:
                a_i = a_d[:, :C]
            a_rows.append(jnp.where(lane_c <= r0 + row_c, a_i, 0.0))
        a.append(jnp.concatenate(a_rows, axis=0).astype(BF16))

    for k in range(nch):
        u_ref[k] = _dot_tn(v16[k], jnp.exp2(b[k][C - 1:C, :] - c[k]).astype(BF16))

    st = st_ref[...]
    for k in range(nch):
        o = _dot(a[k], v16[k]) + _dot_nt((q[k] * jnp.exp2(b[k])).astype(BF16), st.astype(BF16))
        st = st * jnp.exp2(b[k][C - 1:C, :]) + u_ref[k]
        ms = jnp.mean(o * o, axis=-1, keepdims=True)
        y_ref[k * C:(k + 1) * C, :] = o * lax.rsqrt(ms + RMS_EPS) * og * gate_ref[k * C:(k + 1) * C, :]
    st_ref[...] = st


def _hgrn_call(hq, lf, lk, hv, hgate, og, layer, tb):
    bsz, seq, _ = hq.shape
    piece = pl.BlockSpec((None, tb, HG_DK), lambda b, h, i: (b, i, h))
    return pl.pallas_call(
        functools.partial(_hgrn_kernel, tb=tb),
        grid=(bsz, HG_HEADS, seq // tb),
        in_specs=[piece, piece, piece, piece, piece, _of_layer(og, layer)],
        out_specs=piece,
        out_shape=jax.ShapeDtypeStruct((bsz, seq, HG_W), F32),
        scratch_shapes=[pltpu.VMEM((HG_DV, HG_DK), F32), pltpu.VMEM((tb // HG_CHUNK, HG_DV, HG_DK), F32)],
        compiler_params=pltpu.CompilerParams(dimension_semantics=("parallel", "parallel", "arbitrary"),
                                             vmem_limit_bytes=VMEM_LIMIT),
        name="hgrn2",
    )(hq, lf, lk, hv, hgate, og)


def _cmp_kernel(kcm_ref, vcm_ref, pe_ref, w1_ref, w2_ref, g_ref, kc_ref, vc_ref):
    half = CMP_BLOCK // 2
    ncp = kcm_ref.shape[0] // CMP_STRIDE
    for which, (src, out_ref) in enumerate(((kcm_ref, kc_ref), (vcm_ref, vc_ref))):
        first = jnp.zeros((ncp, NSA_KV_HEADS * CMP_HIDDEN), F32)
        second = jnp.zeros((ncp, NSA_KV_HEADS * CMP_HIDDEN), F32)
        for r in range(half):
            tok = src[pl.ds(r, ncp, stride=CMP_STRIDE), :]
            first = first + _dot((tok + pe_ref[which, r:r + 1, :]).astype(BF16), w1_ref[which, r])
            second = second + _dot((tok + pe_ref[which, half + r:half + r + 1, :]).astype(BF16), w1_ref[which, half + r])
        hid = first + pltpu.roll(second, ncp - 1, 0)
        out = _dot(_silu(hid).astype(BF16), w2_ref[which])
        if which == 0:
            for kv in range(NSA_KV_HEADS):
                sl = slice(kv * LANE, (kv + 1) * LANE)
                out_ref[:, sl] = _slot_norm(out[:, sl], g_ref[...]).astype(BF16)
        else:
            head_lane = jnp.bitwise_and(lax.broadcasted_iota(jnp.int32, out.shape, 1), LANE - 1)
            out_ref[...] = jnp.where(head_lane < NSA_DH, out, 1.0).astype(BF16)


def _cmp_call(kcm, vcm, pe, w1, w2, g, layer):
    bsz, seq, _ = kcm.shape
    ncp = seq // CMP_STRIDE
    slots = NSA_KV_HEADS * LANE
    tok = pl.BlockSpec((None, seq, KV_W), lambda b: (b, 0, 0))
    out = pl.BlockSpec((None, ncp, slots), lambda b: (b, 0, 0))
    return pl.pallas_call(
        _cmp_kernel,
        grid=(bsz,),
        in_specs=[tok, tok, _of_layer(pe, layer), _of_layer(w1, layer), _of_layer(w2, layer), _of_layer(g, layer)],
        out_specs=[out, out],
        out_shape=[jax.ShapeDtypeStruct((bsz, ncp, slots), BF16)] * 2,
        compiler_params=pltpu.CompilerParams(dimension_semantics=("parallel",), vmem_limit_bytes=VMEM_LIMIT),
        name="compress",
    )(kcm, vcm, pe, w1, w2, g)


def _score_tiles(q, k, bias):
    pieces = []
    for c in range(k.shape[0] // MXU_COLS):
        s = _dot_nt(q, k[c * MXU_COLS:(c + 1) * MXU_COLS])
        pieces.append(s if bias is None else s + bias[:, c * MXU_COLS:(c + 1) * MXU_COLS])
    return pieces


def _exp_tiles(pieces, m_floor):
    row_max = functools.reduce(jnp.maximum, [jnp.max(s, axis=-1, keepdims=True) for s in pieces])
    m = row_max if m_floor is None else jnp.maximum(m_floor, row_max)
    m_wide = jnp.concatenate([jnp.broadcast_to(m, (pieces[0].shape[0], LANE))] * (MXU_COLS // LANE), axis=1)
    return [jnp.exp2(s - m_wide).astype(BF16) for s in pieces], m


def _pv_tiles(p, v):
    return functools.reduce(lambda a, b: a + b,
                            [_dot(pc, v[c * MXU_COLS:(c + 1) * MXU_COLS]) for c, pc in enumerate(p)])


def _nsa_kernel(qn_ref, qr_ref, kc_ref, vc_ref, ksl_ref, vsl_ref, kwn_ref, vwn_ref, ng_ref, nz_ref,
                ovt_ref, gexp_ref, y_ref, qa_ref, ocmp_ref, acc_s_ref, m_s_ref, acc_w_ref, *, tq, tk):
    G = NSA_G
    R = G * tq
    i = pl.program_id(2)
    t0 = i * tq
    ncp = kc_ref.shape[0]
    nsb = ovt_ref.shape[0] - 8
    topk = min(SLC_TOPK, nsb)
    lane = lax.broadcasted_iota(jnp.int32, (tq, LANE), 1)

    def heads_on_rows(ref):
        return jnp.concatenate([ref[:, g * LANE:(g + 1) * LANE] for g in range(G)], axis=0)

    def all_heads(a):
        return jnp.concatenate([a] * G, axis=0)

    q_pos = t0 + lax.broadcasted_iota(jnp.int32, (tq, 1), 0)

    def position_bias(start, width, window=None):
        dist = q_pos - (start + lax.broadcasted_iota(jnp.int32, (tq, width), 1))
        ok = jnp.where(dist >= 0, 0.0, NEG)
        return all_heads(ok if window is None else jnp.where(dist < window, ok, NEG))

    def _compress_and_select(qn_src, qr_src, base):
        pos_col = base + lax.broadcasted_iota(jnp.int32, (tq, 1), 0)
        c_end = lax.broadcasted_iota(jnp.int32, (tq, ncp), 1) * CMP_STRIDE + (CMP_BLOCK - 1)
        s = _dot_nt(heads_on_rows(qn_src), kc_ref[...]) + all_heads(jnp.where(c_end <= pos_col, 0.0, NEG))
        e = jnp.exp2(s - jnp.max(s, axis=-1, keepdims=True))
        e_hi = e.astype(BF16)
        e_lo = (e - e_hi.astype(F32)).astype(BF16)
        ocmp_ref[...] = _dot(e_hi, vc_ref[...])

        ovt = ovt_ref[...]
        raw_t = _dot_nt(ovt, e_hi) + _dot_nt(ovt, e_lo)
        any_valid_t = jnp.where(base + lax.broadcasted_iota(jnp.int32, (1, tq), 1) >= CMP_BLOCK - 1, 1.0, 0.0)
        imp_t = jnp.zeros((nsb, tq), F32)
        for g in range(G):
            imp_t = imp_t + raw_t[:nsb, g * tq:(g + 1) * tq] * (any_valid_t / raw_t[nsb:nsb + 1, g * tq:(g + 1) * tq])
        j_row = lax.broadcasted_iota(jnp.int32, (nsb, tq), 0)
        cur = jnp.right_shift(base + lax.broadcasted_iota(jnp.int32, (nsb, tq), 1), SLC_SHIFT)
        forced = (j_row == 0) | (j_row == cur) | (j_row == cur - 1)
        score = jnp.where(forced, FORCE_SCORE, jnp.where(j_row <= cur, imp_t, -1.0))
        yield
        ngrp = nsb // 8
        sub8 = lax.broadcasted_iota(jnp.int32, (8, LANE), 0)
        pen_cols, steps = [], 0
        for qt in range(tq // LANE):
            sc = score[:, qt * LANE:(qt + 1) * LANE]
            grp = [sc[8 * a:8 * a + 8] for a in range(ngrp)]
            beats = [jnp.zeros((8, LANE), F32) for _ in range(ngrp)]
            for jp in range(nsb):
                row = sc[jp:jp + 1, :]
                for a in range(ngrp):
                    if a > jp // 8:
                        hit = jnp.where(row >= grp[a], 1.0, 0.0)
                    elif a < jp // 8:
                        hit = jnp.where(row > grp[a], 1.0, 0.0)
                    else:
                        hit = jnp.where(sub8 > jp % 8, jnp.where(row >= grp[a], 1.0, 0.0),
                                        jnp.where(row > grp[a], 1.0, 0.0))
                    beats[a] = beats[a] + hit
                steps += 1
                if steps % (nsb * (tq // LANE) // RANK_STAGES) == 0 and steps < nsb * (tq // LANE):
                    yield
            pen_cols.append(jnp.concatenate([jnp.where(b < topk, 0.0, PENALTY) for b in beats], axis=0))
        pen_t = jnp.concatenate(pen_cols, axis=1)
        zeros_t = jnp.zeros((NSA_DH, tq), F32)
        pad_t = [zeros_t[:LANE - NSA_DH - nsb]] if nsb < LANE - NSA_DH else []
        pen = jnp.concatenate([zeros_t, pen_t] + pad_t, axis=0).T.astype(BF16)
        for g in range(G):
            qa_ref[g * tq:(g + 1) * tq, :] = jnp.where(lane < NSA_DH, qr_src[:, g * LANE:(g + 1) * LANE], pen)

    select = _compress_and_select(qn_ref, qr_ref, t0)

    q4 = heads_on_rows(qr_ref)
    w0 = pl.multiple_of(jnp.maximum(t0 - 2 * tq, 0), tq)
    s_w = _score_tiles(q4, kwn_ref[pl.ds(w0, 3 * tq), :], position_bias(w0, 3 * tq, WINDOW))
    next(select)
    p_w, _ = _exp_tiles(s_w, None)
    next(select)
    acc_w_ref[...] = _pv_tiles(p_w, vwn_ref[pl.ds(w0, 3 * tq), :])
    for _ in select:
        pass

    qa = qa_ref[...]
    top = pl.multiple_of((i // 2) * tk, tk)
    p_top, m_top = _exp_tiles(_score_tiles(qa, ksl_ref[pl.ds(top, tk), :], position_bias(top, tk)), None)
    acc_s_ref[...] = _pv_tiles(p_top, vsl_ref[pl.ds(top, tk), :])
    m_s_ref[...] = jnp.broadcast_to(m_top, (R, LANE))

    def tiles_below(first, count):
        starts = [pl.multiple_of((first + c) * tk, tk) for c in range(count)]
        scores = _score_tiles(qa, ksl_ref[pl.ds(starts[0], tk), :], None)
        for c in range(count):
            cur = scores
            if c + 1 < count:
                scores = _score_tiles(qa, ksl_ref[pl.ds(starts[c + 1], tk), :], None)
            p, m_new = _exp_tiles(cur, m_s_ref[...])
            acc_s_ref[...] = jnp.exp2(m_s_ref[...] - m_new) * acc_s_ref[...] + _pv_tiles(p, vsl_ref[pl.ds(starts[c], tk), :])
            m_s_ref[...] = m_new

    n_below = i // 2
    lax.fori_loop(0, n_below // 4, lambda j, carry: tiles_below(4 * j, 4) or carry, 0)
    lax.fori_loop(0, (n_below % 4) // 2, lambda j, carry: tiles_below(n_below // 4 * 4, 2) or carry, 0)
    lax.fori_loop(0, n_below % 2, lambda j, carry: tiles_below(n_below - 1, 1) or carry, 0)

    sig = _sigmoid(ng_ref[...])
    sig_hi = sig.astype(BF16)
    sig_lo = (sig - sig_hi.astype(F32)).astype(BF16)
    gates = _dot(sig_hi, gexp_ref[...]) + _dot(sig_lo, gexp_ref[...])
    low = lane < NSA_DH
    any_valid = jnp.where(t0 + lax.broadcasted_iota(jnp.int32, (tq, LANE), 0) >= CMP_BLOCK - 1, 1.0, 0.0)

    def normalised(acc, pair):
        even, odd = acc[2 * pair * tq:(2 * pair + 1) * tq, :], acc[(2 * pair + 1) * tq:(2 * pair + 2) * tq, :]
        even_sw, odd_sw = pltpu.roll(even, NSA_DH, 1), pltpu.roll(odd, NSA_DH, 1)
        return jnp.where(low, even, odd_sw) / jnp.where(low, even_sw, odd)

    for pair in range(G // 2):
        cols = slice(pair * LANE, (pair + 1) * LANE)
        gate = lambda branch: gates[:, (branch * (G // 2) + pair) * LANE:(branch * (G // 2) + pair + 1) * LANE]
        o = (gate(0) * any_valid * normalised(ocmp_ref, pair) + gate(1) * normalised(acc_s_ref, pair)
             + gate(2) * normalised(acc_w_ref, pair))
        y_ref[:, cols] = o * _silu(nz_ref[:, cols])


def _nsa_call(qn, qr, kc, vc, ksl, vsl, kwn, vwn, ng, nz, ovt, gexp, tq, tk):
    bsz, seq, _ = qn.shape
    ncp = kc.shape[1]
    rows = NSA_G * tq
    per_kv = lambda nrows, width: pl.BlockSpec((None, nrows, width), lambda b, k, i: (b, 0, k))
    qspec = pl.BlockSpec((None, tq, NSA_G * LANE), lambda b, k, i: (b, i, k))
    yspec = pl.BlockSpec((None, tq, NSA_G * NSA_DH), lambda b, k, i: (b, i, k))
    return pl.pallas_call(
        functools.partial(_nsa_kernel, tq=tq, tk=tk),
        grid=(bsz, NSA_KV_HEADS, seq // tq),
        in_specs=[qspec, qspec, per_kv(ncp, LANE), per_kv(ncp, V_SLOT), per_kv(seq, LANE),
                  per_kv(seq, V_SLOT), per_kv(seq, LANE), per_kv(seq, V_SLOT),
                  pl.BlockSpec((None, tq, LANE), lambda b, k, i: (b, i, 0)), yspec,
                  pl.BlockSpec(ovt.shape, lambda b, k, i: (0, 0)),
                  pl.BlockSpec((None,) + gexp.shape[1:], lambda b, k, i: (k, 0, 0))],
        out_specs=yspec,
        out_shape=jax.ShapeDtypeStruct((bsz, seq, NSA_W), F32),
        scratch_shapes=[pltpu.VMEM((rows, LANE), BF16), pltpu.VMEM((rows, LANE), F32),
                        pltpu.VMEM((rows, V_SLOT), F32), pltpu.VMEM((rows, LANE), F32), pltpu.VMEM((rows, V_SLOT), F32)],
        compiler_params=pltpu.CompilerParams(dimension_semantics=("parallel", "parallel", "parallel"),
                                             vmem_limit_bytes=VMEM_LIMIT),
        name="nsa_attn",
    )(qn, qr, kc, vc, ksl, vsl, kwn, vwn, ng, nz, ovt, gexp)


def _out_kernel(h_ref, yh_ref, yn_ref, p_ref, wo_ref, g_ref, wpg_ref, wpp_ref, o_ref):
    h = h_ref[...] + _dot(yh_ref[...].astype(BF16), wo_ref[0:HG_W, :]) + _dot(yn_ref[...].astype(BF16), wo_ref[HG_W:, :])
    var = jnp.mean(h * h, axis=-1, keepdims=True)
    hn = (h * lax.rsqrt(var + RMS_EPS) * g_ref[...]).astype(BF16)
    gate = _sigmoid(_dot(hn, wpg_ref[...]))
    o_ref[...] = h + gate * _dot(p_ref[...].astype(BF16), wpp_ref[...])


def _out_call(h2, yh, yn, p3, wo, g, wpg, wpp, layer, tm):
    n = h2.shape[0]
    row = lambda width: pl.BlockSpec((tm, width), lambda i: (i, 0))
    return pl.pallas_call(
        _out_kernel,
        grid=(n // tm,),
        in_specs=[row(D_MODEL), row(HG_W), row(NSA_W), pl.BlockSpec((None, tm, PLE_DIM), lambda i: (layer, i, 0)),
                  _of_layer(wo, layer), _of_layer(g, layer), _of_layer(wpg, layer), _of_layer(wpp, layer)],
        out_specs=row(D_MODEL),
        out_shape=jax.ShapeDtypeStruct((n, D_MODEL), F32),
        compiler_params=pltpu.CompilerParams(dimension_semantics=("parallel",), vmem_limit_bytes=VMEM_LIMIT),
        name="out_proj_ple",
    )(h2, yh, yn, p3, wo, g, wpg, wpp)


def _pad_last(a, width):
    return jnp.pad(a, [(0, 0)] * (a.ndim - 1) + [(0, width - a.shape[-1])])


def _layout_w_in(w_in):
    n_gate = 3 * NSA_HEADS
    ngate, nz = w_in[..., C_NG:C_NG + n_gate], w_in[..., C_NG + n_gate:]
    whole = _pad_last(w_in, -(-w_in.shape[-1] // LANE) * LANE).astype(BF16)
    return whole, jnp.concatenate([_pad_last(ngate, LANE), nz], axis=-1).astype(BF16)


def _layout_cmp(pe, w1, w2):
    kvh = NSA_KV_HEADS
    lead = w1.shape[:2]

    def block_diag(a):
        rows = [jnp.concatenate([a if g == k else jnp.zeros_like(a) for k in range(kvh)], axis=-1) for g in range(kvh)]
        return jnp.concatenate(rows, axis=-2)

    w1x = block_diag(w1.astype(BF16).reshape(lead + (CMP_BLOCK, NSA_DH, CMP_HIDDEN)))
    w2x = block_diag(_pad_last(w2, LANE).astype(BF16))
    pe2 = jnp.tile(pe, (1, 1, 1, kvh))
    return pe2, w1x, w2x


def kernel(x, p, norm_g, w_in, hgrn_lb, hgrn_onorm_g, nsa_qnorm_g, nsa_knorm_g, cmp_pe, cmp_w1, cmp_w2, w_out,
           ple_norm_g, w_pg, w_pp):
    bsz, seq, _ = x.shape
    depth = w_in.shape[0]
    n = bsz * seq
    tm = 256
    tq, tk, tb = WINDOW // 2, WINDOW, 2048
    assert seq % tk == 0 and seq // SLC_BLOCK <= LANE - NSA_DH and (seq // SLC_BLOCK) % 8 == 0
    ncp = seq // CMP_STRIDE
    nsb = seq // SLC_BLOCK

    pos = jnp.arange(seq, dtype=F32)
    inv = ROPE_THETA ** (-jnp.arange(0, ROPE_DIM, 2, dtype=F32) / ROPE_DIM)
    ang = pos[:, None] * inv[None, :]
    cos, sin = jnp.cos(ang), jnp.sin(ang)
    half = ROPE_DIM // 2
    per_head = lambda t: jnp.tile(_pad_last(t, NSA_DH), (1, LANE // NSA_DH))
    cos_t = per_head(jnp.concatenate([cos, cos, jnp.ones((seq, NSA_DH - ROPE_DIM), F32)], axis=1))
    sa_t = per_head(-sin)
    sb_t = per_head(jnp.concatenate([jnp.zeros((seq, half), F32), sin], axis=1))

    lb_all = jnp.cumsum(jax.nn.softmax(hgrn_lb.astype(F32), axis=0), axis=0)
    lb_all = lb_all - lb_all[0]
    lbv = jnp.stack([jnp.log(lb_all), jnp.log1p(-lb_all)], axis=1)

    c_tok = jnp.arange(ncp)[:, None] * CMP_STRIDE + jnp.arange(CMP_BLOCK)[None, :]
    overlap = jnp.mean((c_tok[..., None] // SLC_BLOCK == jnp.arange(nsb)).astype(F32), axis=1)
    ovt = jnp.concatenate([overlap.T, jnp.ones((8, ncp), F32)], axis=0).astype(BF16)
    col = jnp.arange(3 * (NSA_G // 2) * LANE)
    src = (col // (NSA_G // 2 * LANE)) * NSA_HEADS + ((col // LANE) % (NSA_G // 2)) * 2 + (col % LANE) // NSA_DH
    gexp = (jnp.arange(LANE)[None, :, None] == src[None, None, :] + NSA_G * jnp.arange(NSA_KV_HEADS)[:, None, None]).astype(BF16)

    w_all, w_tail = _layout_w_in(w_in)
    wo16, wpg16, wpp16 = w_out.astype(BF16), w_pg.astype(BF16), w_pp.astype(BF16)
    row_vec = lambda a: a[:, None, :]
    qg_all = row_vec(jnp.tile(nsa_qnorm_g, (1, LANE // NSA_DH)))
    kg_tiled = jnp.tile(nsa_knorm_g[:, 1:], (1, 1, LANE // NSA_DH))
    kg_cmp = _pad_last(nsa_knorm_g[:, 0:1], LANE)
    cmp_params = _layout_cmp(cmp_pe, cmp_w1, cmp_w2)
    p3 = p.reshape(depth, n, PLE_DIM)

    h = x.reshape(n, D_MODEL)
    for i in range(depth):
        r3 = lambda a: a.reshape(bsz, seq, -1)
        (hq, lf, lk, hv, hgate, qn, qr, kcm, vcm, ksl, vsl, kwn, vwn, ng, nz) = _proj_call(
            h, row_vec(norm_g), w_all, w_tail, cos_t, sa_t, sb_t, qg_all, kg_tiled, lbv, i, seq, 2 * tm)
        y_hg = _hgrn_call(r3(hq), r3(lf), r3(lk), r3(hv), r3(hgate), row_vec(hgrn_onorm_g), i, tb)
        kc, vc = _cmp_call(r3(kcm), r3(vcm), *cmp_params, kg_cmp, i)
        y_nsa = _nsa_call(r3(qn), r3(qr), kc, vc, r3(ksl), r3(vsl), r3(kwn), r3(vwn), r3(ng), r3(nz), ovt, gexp,
                          tq, tk)
        h = _out_call(h, y_hg.reshape(n, HG_W), y_nsa.reshape(n, NSA_W), p3, wo16, row_vec(ple_norm_g), wpg16, wpp16,
                      i, 2 * tm)
    return h.reshape(bsz, seq, D_MODEL)
```

```python
import functools

import jax
import jax.numpy as jnp
from jax import lax
from jax.experimental import pallas as pl
from jax.experimental.pallas import tpu as pltpu

F32 = jnp.float32
BF16 = jnp.bfloat16

D_MODEL = 1024
HG_HEADS = 4
HG_DK = 128
HG_DV = 128
HG_W = HG_HEADS * HG_DV
HG_CHUNK = 64
HG_SUB = 16
NSA_HEADS = 8
NSA_KV_HEADS = 2
NSA_DH = 64
NSA_G = NSA_HEADS // NSA_KV_HEADS
NSA_W = NSA_HEADS * NSA_DH
KV_W = NSA_KV_HEADS * NSA_DH
CMP_BLOCK = 32
CMP_STRIDE = 16
CMP_HIDDEN = 128
SLC_BLOCK = 64
SLC_TOPK = 16
WINDOW = 512
FORCE_SCORE = 1e4
ROPE_THETA = 500000.0
ROPE_DIM = NSA_DH // 4
PLE_DIM = 256
RMS_EPS = 1e-6
IN_SPLITS = (HG_W,) * 5 + (NSA_W,) + (KV_W,) * 6 + (3 * NSA_HEADS, NSA_W)

LANE = 128
SUBLANES = 8
NEG = -1e30
LOG2E = 1.4426950408889634
SLC_SHIFT = SLC_BLOCK.bit_length() - 1
MXU_COLS = 256
PENALTY = -(2.0 ** 100)
RANK_STAGES = 2
VMEM_LIMIT = 56 * 1024 * 1024

V_SLOT = LANE

C_HG = 0
C_Q = C_HG + 5 * HG_W
C_KCM = C_Q + NSA_W
C_KSL = C_KCM + 2 * KV_W
C_KWN = C_KSL + 2 * KV_W
C_NG = C_KWN + 2 * KV_W
C_NZ = C_NG + LANE
C_END = C_NZ + NSA_W


def _dot(a, b):
    return jnp.dot(a, b, preferred_element_type=F32)


def _dot_nt(a, b):
    return lax.dot_general(a, b, (((1,), (1,)), ((), ())), preferred_element_type=F32)


def _dot_tn(a, b):
    return lax.dot_general(a, b, (((0,), (0,)), ((), ())), preferred_element_type=F32)


def _sigmoid(x):
    return 1.0 / (1.0 + jnp.exp(-x))


def _silu(x):
    return x * _sigmoid(x)


def _slot_norm(x, g):
    ms = jnp.sum(x * x, axis=-1, keepdims=True) * (1.0 / NSA_DH)
    return x * lax.rsqrt(ms + RMS_EPS) * g


def _proj_kernel(h_ref, g_ref, w_ref, wtail_ref, cos_ref, sa_ref, sb_ref, qg_ref, kg_ref, lb_ref,
                 hq_ref, lf_ref, lk_ref, hv_ref, hgate_ref, qn_ref, qr_ref, kcm_ref, vcm_ref, ksl_ref, vsl_ref,
                 kwn_ref, vwn_ref, ng_ref, nz_ref, *, tiles_per_seq):
    x = h_ref[...]
    var = jnp.mean(x * x, axis=-1, keepdims=True)
    xn = (x * lax.rsqrt(var + RMS_EPS) * g_ref[...]).astype(BF16)

    def mm(a, b):
        return _dot(xn, w_ref[:, a:b])

    W = HG_W
    tm = x.shape[0]

    def decay_prep(fl, head):
        nrows, row0 = tm, 0
        cols = slice(head * LANE, (head + 1) * LANE)
        loglb, l1mlb = lb_ref[0:1, cols], lb_ref[1:2, cols]
        l1pe = jnp.log(1.0 + jnp.exp(-jnp.abs(fl)))
        cc = l1mlb + (jnp.minimum(fl, 0.0) - l1pe)
        lf = (jnp.maximum(loglb, cc) + jnp.log(1.0 + jnp.exp(-jnp.abs(loglb - cc)))) * LOG2E
        lk = (l1mlb - jnp.maximum(fl, 0.0) - l1pe) * LOG2E
        t3 = lf.reshape(nrows // SUBLANES, SUBLANES, LANE)
        sub = lax.broadcasted_iota(jnp.int32, t3.shape, 1)
        for step in (1, 2, 4):
            t3 = t3 + jnp.where(sub >= step, pltpu.roll(t3, step, 1), 0.0)
        per_chunk = HG_CHUNK // SUBLANES
        for tile in range(nrows // SUBLANES):
            rows = slice(tile * SUBLANES, (tile + 1) * SUBLANES)
            out_rows = slice(row0 + tile * SUBLANES, row0 + (tile + 1) * SUBLANES)
            total = t3[tile, SUBLANES - 1:SUBLANES, :]
            if tile % per_chunk == 0:
                b, run = t3[tile], total
            else:
                b, run = t3[tile] + run, run + total
            lf_ref[out_rows, cols] = b
            lk_ref[out_rows, cols] = b - lk[rows]

    fl_a = mm(C_HG + W, C_HG + W + W // 2)
    qs = mm(C_Q, C_KCM)
    decay_prep(fl_a[:, :LANE], 0)
    kvc = mm(C_KCM, C_KSL)
    kv_s = mm(C_KSL, C_KWN)
    decay_prep(fl_a[:, LANE:], 1)
    fl_b = mm(C_HG + W + W // 2, C_HG + 2 * W)
    kv_w = mm(C_KWN, C_NG)
    gz = _dot(xn, wtail_ref[...])
    decay_prep(fl_b[:, :LANE], 2)
    decay_prep(fl_b[:, LANE:], 3)

    c, sa, sb = cos_ref[...], sa_ref[...], sb_ref[...]
    scale = NSA_DH ** -0.5 * LOG2E
    lane = lax.broadcasted_iota(jnp.int32, (tm, LANE), 1)
    low = lane < NSA_DH
    kcm_ref[...] = kvc[:, :KV_W]
    vcm_ref[...] = kvc[:, KV_W:]
    ng_ref[...] = gz[:, :LANE]
    nz_ref[...] = gz[:, LANE:]

    nq = NSA_HEADS // 2
    tiles = [qs[:, pr * LANE:(pr + 1) * LANE] for pr in range(nq)] + [kv_s[:, :KV_W], kv_w[:, :KV_W]]
    gains = [qg_ref[...] * scale] * nq + [kg_ref[0:1, :], kg_ref[1:2, :]]
    half = len(tiles) // 2
    sq = jnp.concatenate([jnp.concatenate([t * t for t in tiles[:half]], axis=0),
                          jnp.concatenate([t * t for t in tiles[half:]], axis=0)], axis=1)
    same_head = (jnp.right_shift(lax.broadcasted_iota(jnp.int32, (2 * LANE, 2 * LANE), 0), NSA_DH.bit_length() - 1)
                 == jnp.right_shift(lax.broadcasted_iota(jnp.int32, (2 * LANE, 2 * LANE), 1), NSA_DH.bit_length() - 1))
    head_sum = jnp.where(same_head, 1.0, 0.0).astype(BF16)
    sq_hi = sq.astype(BF16)
    sq_lo = (sq - sq_hi.astype(F32)).astype(BF16)
    ssq = _dot(sq_hi, head_sum) + _dot(sq_lo, head_sum)
    ms = [ssq[(j % half) * tm:(j % half + 1) * tm, (j // half) * LANE:(j // half + 1) * LANE] * (1.0 / NSA_DH)
          for j in range(len(tiles))]
    normed = [t * lax.rsqrt(m + RMS_EPS) * g for t, m, g in zip(tiles, ms, gains)]
    hq_ref[...] = mm(C_HG, C_HG + W)
    roped = [t * c + pltpu.roll(t, LANE - ROPE_DIM // 2, 1) * sa + pltpu.roll(t, ROPE_DIM // 2, 1) * sb for t in normed]
    hv_ref[...] = mm(C_HG + 2 * W, C_HG + 3 * W).astype(BF16)

    def head_slots(tile):
        return jnp.where(low, tile, 0.0), jnp.where(low, pltpu.roll(tile, NSA_DH, 1), 0.0)

    pos = (pl.program_id(0) % tiles_per_seq) * tm + lax.broadcasted_iota(jnp.int32, (tm, LANE), 0)
    blk_onehot = jnp.where(lane - NSA_DH == jnp.right_shift(pos, SLC_SHIFT), 1.0, 0.0)
    for pr in range(nq):
        for k, (a, b) in enumerate(zip(head_slots(normed[pr]), head_slots(roped[pr]))):
            sl = slice((2 * pr + k) * LANE, (2 * pr + k + 1) * LANE)
            qn_ref[:, sl] = a.astype(BF16)
            qr_ref[:, sl] = b.astype(BF16)
    for kv, (ks, kw) in enumerate(zip(head_slots(roped[nq]), head_slots(roped[nq + 1]))):
        sl = slice(kv * LANE, (kv + 1) * LANE)
        ksl_ref[:, sl] = jnp.where(low, ks, blk_onehot).astype(BF16)
        kwn_ref[:, sl] = kw.astype(BF16)
    for ref, t in ((vsl_ref, kv_s[:, KV_W:]), (vwn_ref, kv_w[:, KV_W:])):
        for kv, v in enumerate((t, pltpu.roll(t, NSA_DH, 1))):
            ref[:, kv * V_SLOT:(kv + 1) * V_SLOT] = jnp.where(low, v, 1.0).astype(BF16)

    hgate_ref[...] = _sigmoid(mm(C_HG + 3 * W, C_HG + 4 * W)) * _silu(mm(C_HG + 4 * W, C_HG + 5 * W))


def _of_layer(a, layer, **kw):
    return pl.BlockSpec((None,) + a.shape[1:], lambda *_: (layer,) + (0,) * (a.ndim - 1), **kw)


def _proj_call(h2, g, w, w_tail, cos_t, sa_t, sb_t, qg, kg, lbv, layer, seq, tm):
    n = h2.shape[0]
    tpb = seq // tm
    row = lambda width: pl.BlockSpec((tm, width), lambda i: (i, 0))
    tab = pl.BlockSpec((tm, LANE), lambda i: (i % tpb, 0))
    k_slots, v_slots = NSA_KV_HEADS * LANE, NSA_KV_HEADS * V_SLOT
    out_w = [(HG_W, F32), (HG_W, F32), (HG_W, F32), (HG_W, BF16), (HG_W, F32),
             (NSA_HEADS * LANE, BF16), (NSA_HEADS * LANE, BF16), (KV_W, F32), (KV_W, F32),
             (k_slots, BF16), (v_slots, BF16), (k_slots, BF16), (v_slots, BF16), (LANE, F32), (NSA_W, F32)]
    return pl.pallas_call(
        functools.partial(_proj_kernel, tiles_per_seq=tpb),
        grid=(n // tm,),
        in_specs=[row(D_MODEL), _of_layer(g, layer), _of_layer(w, layer, pipeline_mode=pl.Buffered(1)),
                  _of_layer(w_tail, layer, pipeline_mode=pl.Buffered(1)), tab, tab, tab,
                  _of_layer(qg, layer), _of_layer(kg, layer), _of_layer(lbv, layer)],
        out_specs=[row(wd) for wd, _ in out_w],
        out_shape=[jax.ShapeDtypeStruct((n, wd), dt) for wd, dt in out_w],
        compiler_params=pltpu.CompilerParams(dimension_semantics=("parallel",), vmem_limit_bytes=VMEM_LIMIT),
        name="in_proj",
    )(h2, g, w, w_tail, cos_t, sa_t, sb_t, qg, kg, lbv)


def _hgrn_kernel(q_ref, b_ref, c_ref, v_ref, gate_ref, og_ref, y_ref, st_ref, u_ref, *, tb):
    C, SUB, HALF = HG_CHUNK, HG_SUB, HG_SUB // 2
    nch, nblk = tb // C, C // SUB

    @pl.when(pl.program_id(2) == 0)
    def _():
        st_ref[...] = jnp.zeros_like(st_ref)

    og = og_ref[...]
    lane8 = lax.broadcasted_iota(jnp.int32, (HALF, LANE), 1)
    lane_c = lax.broadcasted_iota(jnp.int32, (SUB, C), 1)
    row_c = lax.broadcasted_iota(jnp.int32, (SUB, C), 0)
    left_col = jnp.where(lax.broadcasted_iota(jnp.int32, (2 * HG_DK, 2 * LANE), 1) < LANE, 1.0, 0.0)
    sum_w = jnp.where(lax.broadcasted_iota(jnp.int32, (2 * HG_DK, 2 * LANE), 0) < HG_DK,
                      left_col, 1.0 - left_col).astype(BF16)

    q = [q_ref[k * C:(k + 1) * C, :] for k in range(nch)]
    b = [b_ref[k * C:(k + 1) * C, :] for k in range(nch)]
    c = [c_ref[k * C:(k + 1) * C, :] for k in range(nch)]
    v16 = [v_ref[k * C:(k + 1) * C, :] for k in range(nch)]

    left, right = [], []
    for k in range(nch):
        for blk in range(nblk):
            r0 = blk * SUB
            q_t, q_b = q[k][r0:r0 + HALF], q[k][r0 + HALF:r0 + SUB]
            b_t, b_b = b[k][r0:r0 + HALF], b[k][r0 + HALF:r0 + SUB]
            tops = [q_t * jnp.exp2(b_t - c[k][r0 + s:r0 + s + 1, :]) for s in range(HALF)]
            bots = [q_b * jnp.exp2(b_b - c[k][r0 + s:r0 + s + 1, :]) for s in range(SUB)]
            left += tops + bots[:HALF // 2]
            right += bots[HALF // 2:]
    per_blk = HALF + HALF // 2
    xs = jnp.concatenate([jnp.concatenate(left, axis=0), jnp.concatenate(right, axis=0)], axis=1)
    rs = _dot(xs.astype(BF16), sum_w)

    a = []
    for k in range(nch):
        a_rows = []
        for blk in range(nblk):
            r0 = blk * SUB
            base = (k * nblk + blk) * per_blk * HALF

            def piece(idx, side):
                return rs[base + idx * HALF:base + (idx + 1) * HALF, side * LANE:(side + 1) * LANE]

            a_t = jnp.zeros((HALF, LANE), F32)
            a_b = jnp.zeros((HALF, LANE), F32)
            for s in range(HALF):
                a_t = jnp.where(lane8 == s, piece(s, 0), a_t)
            for s in range(SUB):
                src = piece(HALF + s, 0) if s < HALF // 2 else piece(s - HALF // 2, 1)
                a_b = jnp.where(lane8 == s, src, a_b)
            a_d = jnp.concatenate([a_t, a_b], axis=0)
            if blk > 0:
                ref_b = b[k][r0 - 1:r0, :]
                q_p = (q[k][r0:r0 + SUB] * jnp.exp2(b[k][r0:r0 + SUB] - ref_b)).astype(BF16)
                k_p = jnp.concatenate([jnp.exp2(ref_b - c[k][:r0]), jnp.zeros((C - r0, HG_DK), F32)], axis=0)
                a_i = pltpu.roll(a_d, r0, 1)[:, :C] + _dot_nt(q_p, k_p.astype(BF16))
            else:
                a_i = a_d[:, :C]
            a_rows.append(jnp.where(lane_c <= r0 + row_c, a_i, 0.0))
        a.append(jnp.concatenate(a_rows, axis=0).astype(BF16))

    for k in range(nch):
        u_ref[k] = _dot_tn(v16[k], jnp.exp2(b[k][C - 1:C, :] - c[k]).astype(BF16))

    st = st_ref[...]
    for k in range(nch):
        o = _dot(a[k], v16[k]) + _dot_nt((q[k] * jnp.exp2(b[k])).astype(BF16), st.astype(BF16))
        st = st * jnp.exp2(b[k][C - 1:C, :]) + u_ref[k]
        ms = jnp.mean(o * o, axis=-1, keepdims=True)
        y_ref[k * C:(k + 1) * C, :] = o * lax.rsqrt(ms + RMS_EPS) * og * gate_ref[k * C:(k + 1) * C, :]
    st_ref[...] = st


def _hgrn_call(hq, lf, lk, hv, hgate, og, layer, tb):
    bsz, seq, _ = hq.shape
    piece = pl.BlockSpec((None, tb, HG_DK), lambda b, h, i: (b, i, h))
    return pl.pallas_call(
        functools.partial(_hgrn_kernel, tb=tb),
        grid=(bsz, HG_HEADS, seq // tb),
        in_specs=[piece, piece, piece, piece, piece, _of_layer(og, layer)],
        out_specs=piece,
        out_shape=jax.ShapeDtypeStruct((bsz, seq, HG_W), F32),
        scratch_shapes=[pltpu.VMEM((HG_DV, HG_DK), F32), pltpu.VMEM((tb // HG_CHUNK, HG_DV, HG_DK), F32)],
        compiler_params=pltpu.CompilerParams(dimension_semantics=("parallel", "parallel", "arbitrary"),
                                             vmem_limit_bytes=VMEM_LIMIT),
        name="hgrn2",
    )(hq, lf, lk, hv, hgate, og)


def _cmp_kernel(kcm_ref, vcm_ref, pe_ref, w1_ref, w2_ref, g_ref, kc_ref, vc_ref):
    half = CMP_BLOCK // 2
    ncp = kcm_ref.shape[0] // CMP_STRIDE
    for which, (src, out_ref) in enumerate(((kcm_ref, kc_ref), (vcm_ref, vc_ref))):
        first = jnp.zeros((ncp, NSA_KV_HEADS * CMP_HIDDEN), F32)
        second = jnp.zeros((ncp, NSA_KV_HEADS * CMP_HIDDEN), F32)
        for r in range(half):
            tok = src[pl.ds(r, ncp, stride=CMP_STRIDE), :]
            first = first + _dot((tok + pe_ref[which, r:r + 1, :]).astype(BF16), w1_ref[which, r])
            second = second + _dot((tok + pe_ref[which, half + r:half + r + 1, :]).astype(BF16), w1_ref[which, half + r])
        hid = first + pltpu.roll(second, ncp - 1, 0)
        out = _dot(_silu(hid).astype(BF16), w2_ref[which])
        if which == 0:
            for kv in range(NSA_KV_HEADS):
                sl = slice(kv * LANE, (kv + 1) * LANE)
                out_ref[:, sl] = _slot_norm(out[:, sl], g_ref[...]).astype(BF16)
        else:
            head_lane = jnp.bitwise_and(lax.broadcasted_iota(jnp.int32, out.shape, 1), LANE - 1)
            out_ref[...] = jnp.where(head_lane < NSA_DH, out, 1.0).astype(BF16)


def _cmp_call(kcm, vcm, pe, w1, w2, g, layer):
    bsz, seq, _ = kcm.shape
    ncp = seq // CMP_STRIDE
    slots = NSA_KV_HEADS * LANE
    tok = pl.BlockSpec((None, seq, KV_W), lambda b: (b, 0, 0))
    out = pl.BlockSpec((None, ncp, slots), lambda b: (b, 0, 0))
    return pl.pallas_call(
        _cmp_kernel,
        grid=(bsz,),
        in_specs=[tok, tok, _of_layer(pe, layer), _of_layer(w1, layer), _of_layer(w2, layer), _of_layer(g, layer)],
        out_specs=[out, out],
        out_shape=[jax.ShapeDtypeStruct((bsz, ncp, slots), BF16)] * 2,
        compiler_params=pltpu.CompilerParams(dimension_semantics=("parallel",), vmem_limit_bytes=VMEM_LIMIT),
        name="compress",
    )(kcm, vcm, pe, w1, w2, g)


def _score_tiles(q, k, bias):
    pieces = []
    for c in range(k.shape[0] // MXU_COLS):
        s = _dot_nt(q, k[c * MXU_COLS:(c + 1) * MXU_COLS])
        pieces.append(s if bias is None else s + bias[:, c * MXU_COLS:(c + 1) * MXU_COLS])
    return pieces


def _exp_tiles(pieces, m_floor):
    row_max = functools.reduce(jnp.maximum, [jnp.max(s, axis=-1, keepdims=True) for s in pieces])
    m = row_max if m_floor is None else jnp.maximum(m_floor, row_max)
    m_wide = jnp.concatenate([jnp.broadcast_to(m, (pieces[0].shape[0], LANE))] * (MXU_COLS // LANE), axis=1)
    return [jnp.exp2(s - m_wide).astype(BF16) for s in pieces], m


def _pv_tiles(p, v):
    return functools.reduce(lambda a, b: a + b,
                            [_dot(pc, v[c * MXU_COLS:(c + 1) * MXU_COLS]) for c, pc in enumerate(p)])


def _nsa_kernel(qn_ref, qr_ref, kc_ref, vc_ref, ksl_ref, vsl_ref, kwn_ref, vwn_ref, ng_ref, nz_ref,
                ovt_ref, gexp_ref, y_ref, qa_ref, ocmp_ref, acc_s_ref, m_s_ref, acc_w_ref, *, tq, tk):
    G = NSA_G
    R = G * tq
    i = pl.program_id(2)
    t0 = i * tq
    ncp = kc_ref.shape[0]
    nsb = ovt_ref.shape[0] - 8
    topk = min(SLC_TOPK, nsb)
    lane = lax.broadcasted_iota(jnp.int32, (tq, LANE), 1)

    def heads_on_rows(ref):
        return jnp.concatenate([ref[:, g * LANE:(g + 1) * LANE] for g in range(G)], axis=0)

    def all_heads(a):
        return jnp.concatenate([a] * G, axis=0)

    q_pos = t0 + lax.broadcasted_iota(jnp.int32, (tq, 1), 0)

    def position_bias(start, width, window=None):
        dist = q_pos - (start + lax.broadcasted_iota(jnp.int32, (tq, width), 1))
        ok = jnp.where(dist >= 0, 0.0, NEG)
        return all_heads(ok if window is None else jnp.where(dist < window, ok, NEG))

    def _compress_and_select(qn_src, qr_src, base):
        pos_col = base + lax.broadcasted_iota(jnp.int32, (tq, 1), 0)
        c_end = lax.broadcasted_iota(jnp.int32, (tq, ncp), 1) * CMP_STRIDE + (CMP_BLOCK - 1)
        s = _dot_nt(heads_on_rows(qn_src), kc_ref[...]) + all_heads(jnp.where(c_end <= pos_col, 0.0, NEG))
        e = jnp.exp2(s - jnp.max(s, axis=-1, keepdims=True))
        e_hi = e.astype(BF16)
        e_lo = (e - e_hi.astype(F32)).astype(BF16)
        ocmp_ref[...] = _dot(e_hi, vc_ref[...])

        ovt = ovt_ref[...]
        raw_t = _dot_nt(ovt, e_hi) + _dot_nt(ovt, e_lo)
        any_valid_t = jnp.where(base + lax.broadcasted_iota(jnp.int32, (1, tq), 1) >= CMP_BLOCK - 1, 1.0, 0.0)
        imp_t = jnp.zeros((nsb, tq), F32)
        for g in range(G):
            imp_t = imp_t + raw_t[:nsb, g * tq:(g + 1) * tq] * (any_valid_t / raw_t[nsb:nsb + 1, g * tq:(g + 1) * tq])
        j_row = lax.broadcasted_iota(jnp.int32, (nsb, tq), 0)
        cur = jnp.right_shift(base + lax.broadcasted_iota(jnp.int32, (nsb, tq), 1), SLC_SHIFT)
        forced = (j_row == 0) | (j_row == cur) | (j_row == cur - 1)
        score = jnp.where(forced, FORCE_SCORE, jnp.where(j_row <= cur, imp_t, -1.0))
        yield
        ngrp = nsb // 8
        sub8 = lax.broadcasted_iota(jnp.int32, (8, LANE), 0)
        pen_cols, steps = [], 0
        for qt in range(tq // LANE):
            sc = score[:, qt * LANE:(qt + 1) * LANE]
            grp = [sc[8 * a:8 * a + 8] for a in range(ngrp)]
            beats = [jnp.zeros((8, LANE), F32) for _ in range(ngrp)]
            for jp in range(nsb):
                row = sc[jp:jp + 1, :]
                for a in range(ngrp):
                    if a > jp // 8:
                        hit = jnp.where(row >= grp[a], 1.0, 0.0)
                    elif a < jp // 8:
                        hit = jnp.where(row > grp[a], 1.0, 0.0)
                    else:
                        hit = jnp.where(sub8 > jp % 8, jnp.where(row >= grp[a], 1.0, 0.0),
                                        jnp.where(row > grp[a], 1.0, 0.0))
                    beats[a] = beats[a] + hit
                steps += 1
                if steps % (nsb * (tq // LANE) // RANK_STAGES) == 0 and steps < nsb * (tq // LANE):
                    yield
            pen_cols.append(jnp.concatenate([jnp.where(b < topk, 0.0, PENALTY) for b in beats], axis=0))
        pen_t = jnp.concatenate(pen_cols, axis=1)
        zeros_t = jnp.zeros((NSA_DH, tq), F32)
        pad_t = [zeros_t[:LANE - NSA_DH - nsb]] if nsb < LANE - NSA_DH else []
        pen = jnp.concatenate([zeros_t, pen_t] + pad_t, axis=0).T.astype(BF16)
        for g in range(G):
            qa_ref[g * tq:(g + 1) * tq, :] = jnp.where(lane < NSA_DH, qr_src[:, g * LANE:(g + 1) * LANE], pen)

    select = _compress_and_select(qn_ref, qr_ref, t0)

    q4 = heads_on_rows(qr_ref)
    w0 = pl.multiple_of(jnp.maximum(t0 - 2 * tq, 0), tq)
    s_w = _score_tiles(q4, kwn_ref[pl.ds(w0, 3 * tq), :], position_bias(w0, 3 * tq, WINDOW))
    next(select)
    p_w, _ = _exp_tiles(s_w, None)
    next(select)
    acc_w_ref[...] = _pv_tiles(p_w, vwn_ref[pl.ds(w0, 3 * tq), :])
    for _ in select:
        pass

    qa = qa_ref[...]
    top = pl.multiple_of((i // 2) * tk, tk)
    p_top, m_top = _exp_tiles(_score_tiles(qa, ksl_ref[pl.ds(top, tk), :], position_bias(top, tk)), None)
    acc_s_ref[...] = _pv_tiles(p_top, vsl_ref[pl.ds(top, tk), :])
    m_s_ref[...] = jnp.broadcast_to(m_top, (R, LANE))

    def tiles_below(first, count):
        starts = [pl.multiple_of((first + c) * tk, tk) for c in range(count)]
        scores = _score_tiles(qa, ksl_ref[pl.ds(starts[0], tk), :], None)
        for c in range(count):
            cur = scores
            if c + 1 < count:
                scores = _score_tiles(qa, ksl_ref[pl.ds(starts[c + 1], tk), :], None)
            p, m_new = _exp_tiles(cur, m_s_ref[...])
            acc_s_ref[...] = jnp.exp2(m_s_ref[...] - m_new) * acc_s_ref[...] + _pv_tiles(p, vsl_ref[pl.ds(starts[c], tk), :])
            m_s_ref[...] = m_new

    n_below = i // 2
    lax.fori_loop(0, n_below // 4, lambda j, carry: tiles_below(4 * j, 4) or carry, 0)
    lax.fori_loop(0, (n_below % 4) // 2, lambda j, carry: tiles_below(n_below // 4 * 4, 2) or carry, 0)
    lax.fori_loop(0, n_below % 2, lambda j, carry: tiles_below(n_below - 1, 1) or carry, 0)

    sig = _sigmoid(ng_ref[...])
    sig_hi = sig.astype(BF16)
    sig_lo = (sig - sig_hi.astype(F32)).astype(BF16)
    gates = _dot(sig_hi, gexp_ref[...]) + _dot(sig_lo, gexp_ref[...])
    low = lane < NSA_DH
    any_valid = jnp.where(t0 + lax.broadcasted_iota(jnp.int32, (tq, LANE), 0) >= CMP_BLOCK - 1, 1.0, 0.0)

    def normalised(acc, pair):
        even, odd = acc[2 * pair * tq:(2 * pair + 1) * tq, :], acc[(2 * pair + 1) * tq:(2 * pair + 2) * tq, :]
        even_sw, odd_sw = pltpu.roll(even, NSA_DH, 1), pltpu.roll(odd, NSA_DH, 1)
        return jnp.where(low, even, odd_sw) / jnp.where(low, even_sw, odd)

    for pair in range(G // 2):
        cols = slice(pair * LANE, (pair + 1) * LANE)
        gate = lambda branch: gates[:, (branch * (G // 2) + pair) * LANE:(branch * (G // 2) + pair + 1) * LANE]
        o = (gate(0) * any_valid * normalised(ocmp_ref, pair) + gate(1) * normalised(acc_s_ref, pair)
             + gate(2) * normalised(acc_w_ref, pair))
        y_ref[:, cols] = o * _silu(nz_ref[:, cols])


def _nsa_call(qn, qr, kc, vc, ksl, vsl, kwn, vwn, ng, nz, ovt, gexp, tq, tk):
    bsz, seq, _ = qn.shape
    ncp = kc.shape[1]
    rows = NSA_G * tq
    per_kv = lambda nrows, width: pl.BlockSpec((None, nrows, width), lambda b, k, i: (b, 0, k))
    qspec = pl.BlockSpec((None, tq, NSA_G * LANE), lambda b, k, i: (b, i, k))
    yspec = pl.BlockSpec((None, tq, NSA_G * NSA_DH), lambda b, k, i: (b, i, k))
    return pl.pallas_call(
        functools.partial(_nsa_kernel, tq=tq, tk=tk),
        grid=(bsz, NSA_KV_HEADS, seq // tq),
        in_specs=[qspec, qspec, per_kv(ncp, LANE), per_kv(ncp, V_SLOT), per_kv(seq, LANE),
                  per_kv(seq, V_SLOT), per_kv(seq, LANE), per_kv(seq, V_SLOT),
                  pl.BlockSpec((None, tq, LANE), lambda b, k, i: (b, i, 0)), yspec,
                  pl.BlockSpec(ovt.shape, lambda b, k, i: (0, 0)),
                  pl.BlockSpec((None,) + gexp.shape[1:], lambda b, k, i: (k, 0, 0))],
        out_specs=yspec,
        out_shape=jax.ShapeDtypeStruct((bsz, seq, NSA_W), F32),
        scratch_shapes=[pltpu.VMEM((rows, LANE), BF16), pltpu.VMEM((rows, LANE), F32),
                        pltpu.VMEM((rows, V_SLOT), F32), pltpu.VMEM((rows, LANE), F32), pltpu.VMEM((rows, V_SLOT), F32)],
        compiler_params=pltpu.CompilerParams(dimension_semantics=("parallel", "parallel", "parallel"),
                                             vmem_limit_bytes=VMEM_LIMIT),
        name="nsa_attn",
    )(qn, qr, kc, vc, ksl, vsl, kwn, vwn, ng, nz, ovt, gexp)


def _out_kernel(h_ref, yh_ref, yn_ref, p_ref, wo_ref, g_ref, wpg_ref, wpp_ref, o_ref):
    h = h_ref[...] + _dot(yh_ref[...].astype(BF16), wo_ref[0:HG_W, :]) + _dot(yn_ref[...].astype(BF16), wo_ref[HG_W:, :])
    var = jnp.mean(h * h, axis=-1, keepdims=True)
    hn = (h * lax.rsqrt(var + RMS_EPS) * g_ref[...]).astype(BF16)
    gate = _sigmoid(_dot(hn, wpg_ref[...]))
    o_ref[...] = h + gate * _dot(p_ref[...].astype(BF16), wpp_ref[...])


def _out_call(h2, yh, yn, p3, wo, g, wpg, wpp, layer, tm):
    n = h2.shape[0]
    row = lambda width: pl.BlockSpec((tm, width), lambda i: (i, 0))
    return pl.pallas_call(
        _out_kernel,
        grid=(n // tm,),
        in_specs=[row(D_MODEL), row(HG_W), row(NSA_W), pl.BlockSpec((None, tm, PLE_DIM), lambda i: (layer, i, 0)),
                  _of_layer(wo, layer), _of_layer(g, layer), _of_layer(wpg, layer), _of_layer(wpp, layer)],
        out_specs=row(D_MODEL),
        out_shape=jax.ShapeDtypeStruct((n, D_MODEL), F32),
        compiler_params=pltpu.CompilerParams(dimension_semantics=("parallel",), vmem_limit_bytes=VMEM_LIMIT),
        name="out_proj_ple",
    )(h2, yh, yn, p3, wo, g, wpg, wpp)


def _pad_last(a, width):
    return jnp.pad(a, [(0, 0)] * (a.ndim - 1) + [(0, width - a.shape[-1])])


def _layout_w_in(w_in):
    n_gate = 3 * NSA_HEADS
    ngate, nz = w_in[..., C_NG:C_NG + n_gate], w_in[..., C_NG + n_gate:]
    return w_in.astype(BF16), jnp.concatenate([_pad_last(ngate, LANE), nz], axis=-1).astype(BF16)


def _layout_cmp(pe, w1, w2):
    kvh = NSA_KV_HEADS
    lead = w1.shape[:2]

    def block_diag(a):
        rows = [jnp.concatenate([a if g == k else jnp.zeros_like(a) for k in range(kvh)], axis=-1) for g in range(kvh)]
        return jnp.concatenate(rows, axis=-2)

    w1x = block_diag(w1.astype(BF16).reshape(lead + (CMP_BLOCK, NSA_DH, CMP_HIDDEN)))
    w2x = block_diag(_pad_last(w2, LANE).astype(BF16))
    pe2 = jnp.tile(pe, (1, 1, 1, kvh))
    return pe2, w1x, w2x


def kernel(x, p, norm_g, w_in, hgrn_lb, hgrn_onorm_g, nsa_qnorm_g, nsa_knorm_g, cmp_pe, cmp_w1, cmp_w2, w_out,
           ple_norm_g, w_pg, w_pp):
    bsz, seq, _ = x.shape
    depth = w_in.shape[0]
    n = bsz * seq
    tm = 256
    tq, tk, tb = WINDOW // 2, WINDOW, 2048
    assert seq % tk == 0 and seq // SLC_BLOCK <= LANE - NSA_DH and (seq // SLC_BLOCK) % 8 == 0
    ncp = seq // CMP_STRIDE
    nsb = seq // SLC_BLOCK

    pos = jnp.arange(seq, dtype=F32)
    inv = ROPE_THETA ** (-jnp.arange(0, ROPE_DIM, 2, dtype=F32) / ROPE_DIM)
    ang = pos[:, None] * inv[None, :]
    cos, sin = jnp.cos(ang), jnp.sin(ang)
    half = ROPE_DIM // 2
    per_head = lambda t: jnp.tile(_pad_last(t, NSA_DH), (1, LANE // NSA_DH))
    cos_t = per_head(jnp.concatenate([cos, cos, jnp.ones((seq, NSA_DH - ROPE_DIM), F32)], axis=1))
    sa_t = per_head(-sin)
    sb_t = per_head(jnp.concatenate([jnp.zeros((seq, half), F32), sin], axis=1))

    lb_all = jnp.cumsum(jax.nn.softmax(hgrn_lb.astype(F32), axis=0), axis=0)
    lb_all = lb_all - lb_all[0]
    lbv = jnp.stack([jnp.log(lb_all), jnp.log1p(-lb_all)], axis=1)

    c_tok = jnp.arange(ncp)[:, None] * CMP_STRIDE + jnp.arange(CMP_BLOCK)[None, :]
    overlap = jnp.mean((c_tok[..., None] // SLC_BLOCK == jnp.arange(nsb)).astype(F32), axis=1)
    ovt = jnp.concatenate([overlap.T, jnp.ones((8, ncp), F32)], axis=0).astype(BF16)
    col = jnp.arange(3 * (NSA_G // 2) * LANE)
    src = (col // (NSA_G // 2 * LANE)) * NSA_HEADS + ((col // LANE) % (NSA_G // 2)) * 2 + (col % LANE) // NSA_DH
    gexp = (jnp.arange(LANE)[None, :, None] == src[None, None, :] + NSA_G * jnp.arange(NSA_KV_HEADS)[:, None, None]).astype(BF16)

    w_all, w_tail = _layout_w_in(w_in)
    wo16, wpg16, wpp16 = w_out.astype(BF16), w_pg.astype(BF16), w_pp.astype(BF16)
    row_vec = lambda a: a[:, None, :]
    qg_all = row_vec(jnp.tile(nsa_qnorm_g, (1, LANE // NSA_DH)))
    kg_tiled = jnp.tile(nsa_knorm_g[:, 1:], (1, 1, LANE // NSA_DH))
    kg_cmp = _pad_last(nsa_knorm_g[:, 0:1], LANE)
    cmp_params = _layout_cmp(cmp_pe, cmp_w1, cmp_w2)
    p3 = p.reshape(depth, n, PLE_DIM)

    h = x.reshape(n, D_MODEL)
    for i in range(depth):
        r3 = lambda a: a.reshape(bsz, seq, -1)
        (hq, lf, lk, hv, hgate, qn, qr, kcm, vcm, ksl, vsl, kwn, vwn, ng, nz) = _proj_call(
            h, row_vec(norm_g), w_all, w_tail, cos_t, sa_t, sb_t, qg_all, kg_tiled, lbv, i, seq, 2 * tm)
        y_hg = _hgrn_call(r3(hq), r3(lf), r3(lk), r3(hv), r3(hgate), row_vec(hgrn_onorm_g), i, tb)
        kc, vc = _cmp_call(r3(kcm), r3(vcm), *cmp_params, kg_cmp, i)
        y_nsa = _nsa_call(r3(qn), r3(qr), kc, vc, r3(ksl), r3(vsl), r3(kwn), r3(vwn), r3(ng), r3(nz), ovt, gexp,
                          tq, tk)
        h = _out_call(h, y_hg.reshape(n, HG_W), y_nsa.reshape(n, NSA_W), p3, wo16, row_vec(ple_norm_g), wpg16, wpp16,
                      i, 2 * tm)
    return h.reshape(bsz, seq, D_MODEL)
```
